```python
import math
import jax, jax.numpy as jnp
from jax import lax
import numpy as np

D_MODEL = 1024
BATCH = 8
SEQ = 4096
DEPTH = 4

N_MIXERS = 3
N_NORMS = 7
NORM_EPS = 1e-6
D_FF = 2816
PLE_DIM = 256
CONV_WIDTH = 3
ATT_HEADS = 16
ATT_KV_HEADS = 2
ATT_HEAD_DIM = 64
WINDOW = 128
BLOCK = 128
REL_BUCKETS = 32
REL_MAX_DIST = 128
RET_HEADS = 4
RET_QK_DIM = D_MODEL // RET_HEADS
RET_V_DIM = 2 * RET_QK_DIM
RET_CHUNK = 128
ROPE_BASE = 10000.0

kernel_name = "interleaved_conv_swa_retention_macaron"


def _n_layers_of(kind):
    return (DEPTH - kind + N_MIXERS - 1) // N_MIXERS


def rmsnorm(x, g):
    xf = x.astype(jnp.float32)
    y = xf * lax.rsqrt(jnp.mean(xf * xf, axis=-1, keepdims=True) + NORM_EPS)
    return (y * g.astype(jnp.float32)).astype(x.dtype)


def swiglu(x, w_gu, w_down):
    a, b = jnp.split(x @ w_gu, 2, axis=-1)
    return (jax.nn.silu(a) * b) @ w_down


def short_conv_mixer(x, w_in, conv_w, w_out):
    S = x.shape[1]
    bgate, cgate, v = jnp.split(x @ w_in, 3, axis=-1)
    u = jnp.pad(cgate * v, ((0, 0), (CONV_WIDTH - 1, 0), (0, 0)))
    conv = sum(conv_w[j] * u[:, j:j + S] for j in range(CONV_WIDTH))
    return (bgate * conv) @ w_out


def rel_bucket(dist):
    max_exact = REL_BUCKETS // 2
    d = jnp.maximum(dist, 1).astype(jnp.float32)
    large = max_exact + (jnp.log(d / max_exact) / math.log(REL_MAX_DIST / max_exact)
                         * (REL_BUCKETS - max_exact)).astype(jnp.int32)
    large = jnp.minimum(large, REL_BUCKETS - 1)
    return jnp.where(dist < max_exact, dist, large)


def swa_mixer(x, w_qkv, sinks, w_o, rel_bias):
    Bsz, S, _ = x.shape
    nb = S // BLOCK
    G = ATT_HEADS // ATT_KV_HEADS
    q, k, v = jnp.split(x @ w_qkv, [ATT_HEADS * ATT_HEAD_DIM,
                                    (ATT_HEADS + ATT_KV_HEADS) * ATT_HEAD_DIM], axis=-1)
    q = q.reshape(Bsz, nb, BLOCK, ATT_KV_HEADS, G, ATT_HEAD_DIM)
    k = k.reshape(Bsz, nb, BLOCK, ATT_KV_HEADS, ATT_HEAD_DIM)
    v = v.reshape(Bsz, nb, BLOCK, ATT_KV_HEADS, ATT_HEAD_DIM)
    pad = ((0, 0), (1, 0), (0, 0), (0, 0), (0, 0))
    kb = jnp.concatenate([jnp.pad(k, pad)[:, :-1], k], axis=2)
    vb = jnp.concatenate([jnp.pad(v, pad)[:, :-1], v], axis=2)
    logits = jnp.einsum('bnqhgd,bnkhd->bnhgqk', q, kb).astype(jnp.float32)
    logits = logits * (ATT_HEAD_DIM ** -0.5)
    qi = jnp.arange(BLOCK)[:, None]
    kk = jnp.arange(2 * BLOCK)[None, :]
    dist = qi + BLOCK - kk
    in_window = (dist >= 0) & (dist < WINDOW)
    bias = rel_bias[rel_bucket(jnp.maximum(dist, 0))]
    bias = bias.transpose(2, 0, 1).reshape(ATT_KV_HEADS, G, BLOCK, 2 * BLOCK)
    blk_valid = (jnp.arange(nb)[:, None] > 0) | (kk >= BLOCK)
    mask = in_window[None] & blk_valid[:, None, :]
    logits = jnp.where(mask[None, :, None, None], logits + bias.astype(jnp.float32), -jnp.inf)
    sink = sinks.astype(jnp.float32).reshape(ATT_KV_HEADS, G)[None, None, :, :, None, None]
    m = jnp.maximum(jnp.max(logits, axis=-1, keepdims=True), sink)
    e = jnp.exp(logits - m)
    probs = e / (jnp.sum(e, axis=-1, keepdims=True) + jnp.exp(sink - m))
    out = jnp.einsum('bnhgqk,bnkhd->bnqhgd', probs.astype(vb.dtype), vb)
    return out.reshape(Bsz, S, ATT_HEADS * ATT_HEAD_DIM) @ w_o


def rotary(x, pos):
    half = x.shape[-1] // 2
    inv = ROPE_BASE ** (-jnp.arange(half, dtype=jnp.float32) / half)
    ang = pos[:, None].astype(jnp.float32) * inv[None, :]
    cos = jnp.cos(ang)[:, None, :]
    sin = jnp.sin(ang)[:, None, :]
    x1, x2 = x[..., :half], x[..., half:]
    return jnp.concatenate([x1 * cos - x2 * sin, x1 * sin + x2 * cos], axis=-1)


def retention_mixer(x, w_qkvg, w_o):
    Bsz, S, _ = x.shape
    H, dk, dv, C = RET_HEADS, RET_QK_DIM, RET_V_DIM, RET_CHUNK
    nc = S // C
    f32 = jnp.float32
    q, k, v, g = jnp.split(x @ w_qkvg, [H * dk, 2 * H * dk, 2 * H * dk + H * dv], axis=-1)
    pos = jnp.arange(S)
    q = rotary(q.reshape(Bsz, S, H, dk).astype(f32), pos)
    k = rotary(k.reshape(Bsz, S, H, dk).astype(f32), pos) * (dk ** -0.5)
    v = v.reshape(Bsz, S, H, dv).astype(f32)
    log_g = jnp.log(1.0 - 2.0 ** (-5.0 - jnp.arange(H, dtype=f32)))
    idx = jnp.arange(C, dtype=f32)
    diff = idx[:, None] - idx[None, :]
    decay_mask = jnp.where(diff >= 0, jnp.exp(log_g[:, None, None] * jnp.maximum(diff, 0.0)), 0.0)
    q_decay = jnp.exp(log_g[:, None] * (idx + 1.0))[..., None]
    k_decay = jnp.exp(log_g[:, None] * (C - 1.0 - idx))[..., None]
    chunk_decay = jnp.exp(log_g * C)[:, None, None]

    def to_chunks(t):
        return t.reshape(Bsz, nc, C, H, t.shape[-1]).transpose(1, 0, 3, 2, 4)

    def step(state, inp):
        qi, ki, vi = inp
        inner = jnp.einsum('bhqd,bhkd->bhqk', qi, ki) * decay_mask
        o = (jnp.einsum('bhqk,bhkv->bhqv', inner, vi)
             + jnp.einsum('bhqd,bhdv->bhqv', qi, state) * q_decay)
        state = state * chunk_decay + jnp.einsum('bhkd,bhkv->bhdv', ki * k_decay, vi)
        return state, o

    state0 = jnp.zeros((Bsz, H, dk, dv), f32)
    _, o = lax.scan(step, state0, (to_chunks(q), to_chunks(k), to_chunks(v)))
    o = o.transpose(1, 0, 3, 2, 4).reshape(Bsz, S, H, dv)
    mu = jnp.mean(o, axis=-1, keepdims=True)
    var = jnp.mean(jnp.square(o - mu), axis=-1, keepdims=True)
    o = ((o - mu) * lax.rsqrt(var + NORM_EPS)).reshape(Bsz, S, H * dv).astype(x.dtype)
    return (jax.nn.silu(g) * o) @ w_o


def setup_inputs(seed: int = 0) -> dict:
    key = jax.random.key(seed)
    ks = jax.random.split(key, 20)
    f32 = jnp.float32
    nA, nB, nC = _n_layers_of(0), _n_layers_of(1), _n_layers_of(2)

    def w(k, shape, fan_in):
        return jax.random.normal(k, shape, f32) * (fan_in ** -0.5)

    return {
        "x": jax.random.normal(ks[0], (BATCH, SEQ, D_MODEL), f32),
        "p": jax.random.normal(ks[1], (DEPTH, BATCH, SEQ, PLE_DIM), f32),
        "norm_g": 1.0 + 0.05 * jax.random.normal(ks[2], (DEPTH, N_NORMS, D_MODEL), f32),
        "ffn_w_gu": w(ks[3], (DEPTH, 2, D_MODEL, 2 * D_FF), D_MODEL),
        "ffn_w_down": w(ks[4], (DEPTH, 2, D_FF, D_MODEL), D_FF),
        "ple_w_proj": w(ks[5], (DEPTH, PLE_DIM, D_MODEL), PLE_DIM),
        "ple_w_gate": w(ks[6], (DEPTH, D_MODEL, D_MODEL), D_MODEL),
        "rel_bias": 0.5 * jax.random.normal(ks[7], (REL_BUCKETS, ATT_HEADS), f32),
        "conv_w_in": w(ks[8], (nA, D_MODEL, 3 * D_MODEL), D_MODEL),
        "conv_w": w(ks[9], (nA, CONV_WIDTH, D_MODEL), CONV_WIDTH),
        "conv_w_out": w(ks[10], (nA, D_MODEL, D_MODEL), D_MODEL),
        "swa_w_qkv": w(ks[11], (nB, D_MODEL, (ATT_HEADS + 2 * ATT_KV_HEADS) * ATT_HEAD_DIM), D_MODEL),
        "swa_sinks": 0.5 * jax.random.normal(ks[12], (nB, ATT_HEADS), f32),
        "swa_w_o": w(ks[13], (nB, ATT_HEADS * ATT_HEAD_DIM, D_MODEL), ATT_HEADS * ATT_HEAD_DIM),
        "ret_w_qkvg": w(ks[14], (nC, D_MODEL, 2 * RET_HEADS * RET_QK_DIM + 2 * RET_HEADS * RET_V_DIM), D_MODEL),
        "ret_w_o": w(ks[15], (nC, RET_HEADS * RET_V_DIM, D_MODEL), RET_HEADS * RET_V_DIM),
    }


def reference(x, p, norm_g, ffn_w_gu, ffn_w_down, ple_w_proj, ple_w_gate, rel_bias,
              conv_w_in, conv_w, conv_w_out, swa_w_qkv, swa_sinks, swa_w_o,
              ret_w_qkvg, ret_w_o):
    for i in range(DEPTH):
        kind, j = i % N_MIXERS, i // N_MIXERS
        g = norm_g[i]
        x = x + 0.5 * rmsnorm(swiglu(rmsnorm(x, g[0]), ffn_w_gu[i, 0], ffn_w_down[i, 0]), g[1])
        h = rmsnorm(x, g[2])
        if kind == 0:
            h = short_conv_mixer(h, conv_w_in[j], conv_w[j], conv_w_out[j])
        elif kind == 1:
            h = swa_mixer(h, swa_w_qkv[j], swa_sinks[j], swa_w_o[j], rel_bias)
        else:
            h = retention_mixer(h, ret_w_qkvg[j], ret_w_o[j])
        x = x + rmsnorm(h, g[3])
        x = x + 0.5 * rmsnorm(swiglu(rmsnorm(x, g[4]), ffn_w_gu[i, 1], ffn_w_down[i, 1]), g[5])
        gate = jax.nn.sigmoid(rmsnorm(x, g[6]) @ ple_w_gate[i])
        x = x + gate * (p[i] @ ple_w_proj[i])
    return x
```

```python
import functools
import math

import jax
import jax.numpy as jnp
from jax import lax
from jax.experimental import pallas as pl
from jax.experimental.pallas import tpu as pltpu

N_MIXERS = 3
N_NORMS = 7
NORM_EPS = 1e-6
CONV_WIDTH = 3
ATT_HEADS = 16
ATT_KV_HEADS = 2
ATT_HEAD_DIM = 64
ATT_GROUP = ATT_HEADS // ATT_KV_HEADS
WINDOW = 128
BLOCK = 128
REL_BUCKETS = 32
REL_MAX_DIST = 128
RET_HEADS = 4
RET_CHUNK = 128
ROPE_BASE = 10000.0

LANES = 128
SUBLANES = 8
VMEM_LIMIT_BYTES = 56 * 1024 * 1024

F32 = jnp.float32
BF16 = jnp.bfloat16


def _rms(x, g):
    ms = jnp.mean(x * x, axis=-1, keepdims=True)
    return x * lax.rsqrt(ms + NORM_EPS) * g


def _dot(a, b):
    return jnp.dot(a, b, preferred_element_type=F32)


def _dot_nt(a, b):
    return lax.dot_general(a, b, (((1,), (1,)), ((), ())), preferred_element_type=F32)


def _dot_tn(a, b):
    return lax.dot_general(a, b, (((0,), (0,)), ((), ())), preferred_element_type=F32)


def _resident(shape):
    nd = len(shape)
    return pl.BlockSpec(shape, lambda *_: (0,) * nd, pipeline_mode=pl.Buffered(1))


def _gain_spec_d(row, d):
    return pl.BlockSpec((None, 1, d), lambda *_: (row, 0, 0))


def _params(n_grid):
    return pltpu.CompilerParams(
        dimension_semantics=("arbitrary",) * n_grid,
        vmem_limit_bytes=VMEM_LIMIT_BYTES,
    )


def _ffn_kernel(x_ref, gpre_ref, gpost_ref, wgu_ref, wd_ref, o_ref, *, d_ff, n_chunks):
    x = x_ref[...]
    xn = _rms(x, gpre_ref[...]).astype(BF16)
    fc = d_ff // n_chunks
    y = None
    for c in range(n_chunks):
        a = _dot(xn, wgu_ref[:, c * fc:(c + 1) * fc])
        b = _dot(xn, wgu_ref[:, d_ff + c * fc:d_ff + (c + 1) * fc])
        act = (jax.nn.silu(a) * b).astype(BF16)
        part = _dot(act, wd_ref[c * fc:(c + 1) * fc, :])
        y = part if y is None else y + part
    o_ref[...] = x + 0.5 * _rms(y, gpost_ref[...])


def _ffn(x2, gains, row_pre, row_post, w_gu, w_down, *, tm, n_chunks):
    t, d = x2.shape
    d_ff = w_down.shape[0]
    tile = pl.BlockSpec((tm, d), lambda i: (i, 0))
    return pl.pallas_call(
        functools.partial(_ffn_kernel, d_ff=d_ff, n_chunks=n_chunks),
        grid=(t // tm,),
        in_specs=[tile, _gain_spec_d(row_pre, d), _gain_spec_d(row_post, d),
                  _resident(w_gu.shape), _resident(w_down.shape)],
        out_specs=tile,
        out_shape=jax.ShapeDtypeStruct((t, d), F32),
        compiler_params=_params(1),
        name="ffn",
    )(x2, gains, gains, w_gu, w_down)


def _ple_kernel(x_ref, p_ref, g_ref, wg_ref, wp_ref, o_ref):
    x = x_ref[...]
    xn = _rms(x, g_ref[...]).astype(BF16)
    gate = jax.nn.sigmoid(_dot(xn, wg_ref[...]))
    proj = _dot(p_ref[...].astype(BF16), wp_ref[...])
    o_ref[...] = x + gate * proj


def _ple(x2, p3, layer, gains, row, w_gate, w_proj, *, tm):
    t, d = x2.shape
    pd = p3.shape[-1]
    tile = pl.BlockSpec((tm, d), lambda i: (i, 0))
    return pl.pallas_call(
        _ple_kernel,
        grid=(t // tm,),
        in_specs=[tile, pl.BlockSpec((None, tm, pd), lambda i: (layer, i, 0)),
                  _gain_spec_d(row, d), _resident(w_gate.shape), _resident(w_proj.shape)],
        out_specs=tile,
        out_shape=jax.ShapeDtypeStruct((t, d), F32),
        compiler_params=_params(1),
        name="ple",
    )(x2, p3, gains, w_gate, w_proj)


def _conv_kernel(x_ref, gpre_ref, gpost_ref, win_ref, cw_ref, wout_ref, o_ref, u_buf):
    tm, d = x_ref.shape
    x = x_ref[...]
    h = _rms(x, gpre_ref[...]).astype(BF16)
    proj = _dot(h, win_ref[...])
    bgate, cgate, v = proj[:, :d], proj[:, d:2 * d], proj[:, 2 * d:]
    u = cgate * v

    @pl.when(pl.program_id(1) == 0)
    def _():
        u_buf[0:SUBLANES, :] = jnp.zeros((SUBLANES, d), F32)

    u_buf[SUBLANES:, :] = u
    cw = cw_ref[...]
    conv = (cw[2:3, :] * u
            + cw[1:2, :] * u_buf[SUBLANES - 1:SUBLANES - 1 + tm, :]
            + cw[0:1, :] * u_buf[SUBLANES - 2:SUBLANES - 2 + tm, :])
    u_buf[0:SUBLANES, :] = u[tm - SUBLANES:, :]
    y = _dot((bgate * conv).astype(BF16), wout_ref[...])
    o_ref[...] = x + _rms(y, gpost_ref[...])


def _conv_mixer(x3, gains, row_pre, row_post, w_in, conv_w, w_out, *, tm):
    b, s, d = x3.shape
    tile = pl.BlockSpec((None, tm, d), lambda i, j: (i, j, 0))
    return pl.pallas_call(
        _conv_kernel,
        grid=(b, s // tm),
        in_specs=[tile, _gain_spec_d(row_pre, d), _gain_spec_d(row_post, d),
                  _resident(w_in.shape), _resident(conv_w.shape), _resident(w_out.shape)],
        out_specs=tile,
        out_shape=jax.ShapeDtypeStruct((b, s, d), F32),
        scratch_shapes=[pltpu.VMEM((tm + SUBLANES, d), F32)],
        compiler_params=_params(2),
        name="conv_mixer",
    )(x3, gains, gains, w_in, conv_w, w_out)


def _rel_bucket(dist):
    max_exact = REL_BUCKETS // 2
    dd = jnp.maximum(dist, 1).astype(F32)
    large = max_exact + (jnp.log(dd / max_exact) / math.log(REL_MAX_DIST / max_exact)
                         * (REL_BUCKETS - max_exact)).astype(jnp.int32)
    large = jnp.minimum(large, REL_BUCKETS - 1)
    return jnp.where(dist < max_exact, dist, large)


def _bias_kernel(rb_ref, bucket_ref, valid_ref, o_ref):
    bucket = bucket_ref[...]
    valid_all = valid_ref[...] > 0
    kk = lax.broadcasted_iota(jnp.int32, bucket.shape, 1)
    valid_first = valid_all & (kk >= BLOCK)
    hits = [bucket == bkt for bkt in range(REL_BUCKETS)]
    neg = jnp.full(bucket.shape, -jnp.inf, F32)
    for h in range(ATT_HEADS):
        acc = jnp.zeros(bucket.shape, F32)
        for bkt in range(REL_BUCKETS):
            acc = jnp.where(hits[bkt], rb_ref[bkt, h], acc)
        o_ref[0, h] = jnp.where(valid_first, acc, neg)
        o_ref[1, h] = jnp.where(valid_all, acc, neg)


def _bias_tables(rel_bias):
    qi = jnp.arange(BLOCK)[:, None]
    kk = jnp.arange(2 * BLOCK)[None, :]
    dist = qi + BLOCK - kk
    valid = ((dist >= 0) & (dist < WINDOW)).astype(jnp.int32)
    bucket = _rel_bucket(jnp.maximum(dist, 0)).astype(jnp.int32)
    out = pl.pallas_call(
        _bias_kernel,
        in_specs=[pl.BlockSpec(memory_space=pltpu.SMEM),
                  pl.BlockSpec(memory_space=pltpu.VMEM),
                  pl.BlockSpec(memory_space=pltpu.VMEM)],
        out_specs=pl.BlockSpec(memory_space=pltpu.VMEM),
        out_shape=jax.ShapeDtypeStruct((2, ATT_HEADS, BLOCK, 2 * BLOCK), F32),
        name="rel_bias_table",
    )(rel_bias, bucket, valid)
    return out.reshape(2, ATT_KV_HEADS, ATT_GROUP * BLOCK, 2 * BLOCK)


def _swa_kernel(x_ref, gpre_ref, gpost_ref, wqkv_ref, bias_ref, sink_ref, wo_ref, o_ref,
                kk_buf, vv_buf, att_buf):
    tm, d = x_ref.shape
    n_blk = tm // BLOCK
    dq = ATT_HEADS * ATT_HEAD_DIM
    dkv = ATT_KV_HEADS * ATT_HEAD_DIM
    first_tile = pl.program_id(1) == 0

    x = x_ref[...]
    h = _rms(x, gpre_ref[...]).astype(BF16)
    qkv = _dot(h, wqkv_ref[...])
    q = (qkv[:, :dq] * (ATT_HEAD_DIM ** -0.5)).astype(BF16)
    k = qkv[:, dq:dq + dkv]
    v = qkv[:, dq + dkv:]

    lane = lax.broadcasted_iota(jnp.int32, (tm, LANES), 1)
    low = lane < ATT_HEAD_DIM
    k_sw = pltpu.roll(k, ATT_HEAD_DIM, 1)
    v_sw = pltpu.roll(v, ATT_HEAD_DIM, 1)

    @pl.when(first_tile)
    def _():
        kk_buf[:, 0:BLOCK, :] = jnp.zeros((ATT_KV_HEADS, BLOCK, LANES), BF16)
        vv_buf[:, 0:BLOCK, :] = jnp.zeros((ATT_KV_HEADS, BLOCK, LANES), BF16)

    kk_buf[0, BLOCK:, :] = jnp.where(low, k, k_sw).astype(BF16)
    kk_buf[1, BLOCK:, :] = jnp.where(low, k_sw, k).astype(BF16)
    vv_buf[0, BLOCK:, :] = jnp.where(low, v, v_sw).astype(BF16)
    vv_buf[1, BLOCK:, :] = jnp.where(low, v_sw, v).astype(BF16)

    low_b = lax.broadcasted_iota(jnp.int32, (BLOCK, LANES), 1) < ATT_HEAD_DIM
    zero = jnp.zeros((BLOCK, LANES), BF16)
    pairs = ATT_GROUP // 2
    for n in range(n_blk):
        rows = slice(n * BLOCK, (n + 1) * BLOCK)
        variant = jnp.where(first_tile, 0, 1) if n == 0 else 1
        for hk in range(ATT_KV_HEADS):
            parts = []
            for j in range(pairs):
                col = (hk * pairs + j) * LANES
                q2 = q[rows, col:col + LANES]
                parts.append(jnp.where(low_b, q2, zero))
                parts.append(jnp.where(low_b, zero, q2))
            qs = jnp.concatenate(parts, axis=0)
            kb = kk_buf[hk, n * BLOCK:(n + 2) * BLOCK, :]
            vb = vv_buf[hk, n * BLOCK:(n + 2) * BLOCK, :]
            logits = _dot_nt(qs, kb) + bias_ref[variant, hk]
            sink = sink_ref[hk]
            m = jnp.maximum(jnp.max(logits, axis=-1, keepdims=True), sink)
            e = jnp.exp(logits - m)
            probs = e / (jnp.sum(e, axis=-1, keepdims=True) + jnp.exp(sink - m))
            r = _dot(probs.astype(BF16), vb)
            for j in range(pairs):
                col = (hk * pairs + j) * LANES
                even = r[(2 * j) * BLOCK:(2 * j + 1) * BLOCK, :]
                odd = r[(2 * j + 1) * BLOCK:(2 * j + 2) * BLOCK, :]
                att_buf[rows, col:col + LANES] = jnp.where(low_b, even, odd).astype(BF16)

    kk_buf[:, 0:BLOCK, :] = kk_buf[:, tm:tm + BLOCK, :]
    vv_buf[:, 0:BLOCK, :] = vv_buf[:, tm:tm + BLOCK, :]
    y = _dot(att_buf[...], wo_ref[...])
    o_ref[...] = x + _rms(y, gpost_ref[...])


def _swa_mixer(x3, gains, row_pre, row_post, w_qkv, bias_tab, sinks, w_o, *, tm):
    b, s, d = x3.shape
    tile = pl.BlockSpec((None, tm, d), lambda i, j: (i, j, 0))
    sink_col = jnp.repeat(sinks.astype(F32).reshape(ATT_KV_HEADS, ATT_GROUP), BLOCK, axis=1)[..., None]
    return pl.pallas_call(
        _swa_kernel,
        grid=(b, s // tm),
        in_specs=[tile, _gain_spec_d(row_pre, d), _gain_spec_d(row_post, d),
                  _resident(w_qkv.shape), _resident(bias_tab.shape), _resident(sink_col.shape),
                  _resident(w_o.shape)],
        out_specs=tile,
        out_shape=jax.ShapeDtypeStruct((b, s, d), F32),
        scratch_shapes=[pltpu.VMEM((ATT_KV_HEADS, tm + BLOCK, LANES), BF16),
                        pltpu.VMEM((ATT_KV_HEADS, tm + BLOCK, LANES), BF16),
                        pltpu.VMEM((tm, ATT_HEADS * ATT_HEAD_DIM), BF16)],
        compiler_params=_params(2),
        name="swa_mixer",
    )(x3, gains, gains, w_qkv, bias_tab, sink_col, w_o)


def _ret_tables(s, dk):
    f32 = F32
    half = dk // 2
    inv = ROPE_BASE ** (-jnp.arange(half, dtype=f32) / half)
    ang = jnp.arange(s)[:, None].astype(f32) * inv[None, :]
    c = RET_CHUNK
    log_g = jnp.log(1.0 - 2.0 ** (-5.0 - jnp.arange(RET_HEADS, dtype=f32)))
    idx = jnp.arange(c, dtype=f32)
    diff = idx[:, None] - idx[None, :]
    decay_mask = jnp.where(diff >= 0, jnp.exp(log_g[:, None, None] * jnp.maximum(diff, 0.0)), 0.0)
    q_decay = jnp.exp(log_g[:, None] * (idx + 1.0))[..., None]
    k_decay = jnp.exp(log_g[:, None] * (c - 1.0 - idx))[..., None]
    chunk_decay = jnp.exp(log_g * c)
    return jnp.cos(ang), jnp.sin(ang), decay_mask, q_decay, k_decay, chunk_decay


def _ret_kernel(x_ref, gpre_ref, gpost_ref, w_ref, cos_ref, sin_ref, dmask_ref, qdec_ref,
                kdec_ref, cdec_ref, wo_ref, o_ref, state_ref, gated_buf, *, dk, dv):
    tm, d = x_ref.shape
    n_chunk = tm // RET_CHUNK
    half = dk // 2
    hq = RET_HEADS * dk

    @pl.when(pl.program_id(1) == 0)
    def _():
        state_ref[...] = jnp.zeros(state_ref.shape, F32)

    x = x_ref[...]
    h = _rms(x, gpre_ref[...]).astype(BF16)
    proj = _dot(h, w_ref[...])
    cos = cos_ref[...]
    sin = sin_ref[...]

    def rot(t):
        t1, t2 = t[:, :half], t[:, half:]
        return jnp.concatenate([t1 * cos - t2 * sin, t1 * sin + t2 * cos], axis=-1)

    for hd in range(RET_HEADS):
        q = rot(proj[:, hd * dk:(hd + 1) * dk])
        k = rot(proj[:, hq + hd * dk:hq + (hd + 1) * dk]) * (dk ** -0.5)
        v = proj[:, 2 * hq + hd * dv:2 * hq + (hd + 1) * dv]
        g = proj[:, 2 * hq + RET_HEADS * dv + hd * dv:2 * hq + RET_HEADS * dv + (hd + 1) * dv]
        dmask = dmask_ref[hd]
        qdec = qdec_ref[hd]
        kdec = kdec_ref[hd]
        cdec = cdec_ref[hd]
        state = state_ref[hd]
        for c in range(n_chunk):
            rows = slice(c * RET_CHUNK, (c + 1) * RET_CHUNK)
            qc = q[rows].astype(BF16)
            kc = k[rows]
            vc = v[rows].astype(BF16)
            inner = _dot_nt(qc, kc.astype(BF16)) * dmask
            o = _dot(inner.astype(BF16), vc) + _dot(qc, state.astype(BF16)) * qdec
            state = state * cdec + _dot_tn((kc * kdec).astype(BF16), vc)
            mu = jnp.mean(o, axis=-1, keepdims=True)
            oc = o - mu
            var = jnp.mean(oc * oc, axis=-1, keepdims=True)
            on = oc * lax.rsqrt(var + NORM_EPS)
            gated_buf[rows, hd * dv:(hd + 1) * dv] = (jax.nn.silu(g[rows]) * on).astype(BF16)
        state_ref[hd] = state

    y = _dot(gated_buf[...], wo_ref[...])
    o_ref[...] = x + _rms(y, gpost_ref[...])


def _ret_mixer(x3, gains, row_pre, row_post, w_qkvg, w_o, *, tm):
    b, s, d = x3.shape
    dk = d // RET_HEADS
    dv = 2 * dk
    cos, sin, dmask, qdec, kdec, cdec = _ret_tables(s, dk)
    cdec = cdec.reshape(RET_HEADS, 1, 1)
    tile = pl.BlockSpec((None, tm, d), lambda i, j: (i, j, 0))
    rope = pl.BlockSpec((tm, dk // 2), lambda i, j: (j, 0))
    return pl.pallas_call(
        functools.partial(_ret_kernel, dk=dk, dv=dv),
        grid=(b, s // tm),
        in_specs=[tile, _gain_spec_d(row_pre, d), _gain_spec_d(row_post, d),
                  _resident(w_qkvg.shape), rope, rope, _resident(dmask.shape),
                  _resident(qdec.shape), _resident(kdec.shape), _resident(cdec.shape),
                  _resident(w_o.shape)],
        out_specs=tile,
        out_shape=jax.ShapeDtypeStruct((b, s, d), F32),
        scratch_shapes=[pltpu.VMEM((RET_HEADS, dk, dv), F32),
                        pltpu.VMEM((tm, RET_HEADS * dv), BF16)],
        compiler_params=_params(2),
        name="ret_mixer",
    )(x3, gains, gains, w_qkvg, cos, sin, dmask, qdec, kdec, cdec, w_o)


def _tile(n, want):
    t = min(n, want)
    assert n % t == 0, (n, t)
    return t


def kernel(x, p, norm_g, ffn_w_gu, ffn_w_down, ple_w_proj, ple_w_gate, rel_bias, conv_w_in, conv_w,
           conv_w_out, swa_w_qkv, swa_sinks, swa_w_o, ret_w_qkvg, ret_w_o):
    depth = norm_g.shape[0]
    b, s, d = x.shape
    t = b * s
    gains = norm_g.astype(F32).reshape(depth * N_NORMS, 1, d)
    p3 = p.reshape(depth, t, p.shape[-1])
    bf = lambda w: w.astype(BF16)
    bias_tab = _bias_tables(rel_bias.astype(F32)) if depth > 1 else None

    tm_ffn = _tile(t, 512)
    tm_ple = _tile(t, 1024)
    tm_conv = _tile(s, 512)
    tm_swa = _tile(s, 512)
    tm_ret = _tile(s, 512)

    for i in range(depth):
        kind, j = i % N_MIXERS, i // N_MIXERS
        r = i * N_NORMS
        x2 = _ffn(x.reshape(t, d), gains, r + 0, r + 1, bf(ffn_w_gu[i, 0]), bf(ffn_w_down[i, 0]),
                  tm=tm_ffn, n_chunks=2)
        x = x2.reshape(b, s, d)
        if kind == 0:
            x = _conv_mixer(x, gains, r + 2, r + 3, bf(conv_w_in[j]), conv_w[j].astype(F32),
                            bf(conv_w_out[j]), tm=tm_conv)
        elif kind == 1:
            x = _swa_mixer(x, gains, r + 2, r + 3, bf(swa_w_qkv[j]), bias_tab, swa_sinks[j],
                           bf(swa_w_o[j]), tm=tm_swa)
        else:
            x = _ret_mixer(x, gains, r + 2, r + 3, bf(ret_w_qkvg[j]), bf(ret_w_o[j]), tm=tm_ret)
        x2 = _ffn(x.reshape(t, d), gains, r + 4, r + 5, bf(ffn_w_gu[i, 1]), bf(ffn_w_down[i, 1]),
                  tm=tm_ffn, n_chunks=2)
        x2 = _ple(x2, p3, i, gains, r + 6, bf(ple_w_gate[i]), bf(ple_w_proj[i]), tm=tm_ple)
        x = x2.reshape(b, s, d)
    return x
```

```python
import functools
import math

import jax
import jax.numpy as jnp
from jax import lax
from jax.experimental import pallas as pl
from jax.experimental.pallas import tpu as pltpu

N_MIXERS = 3
N_NORMS = 7
NORM_EPS = 1e-6
CONV_WIDTH = 3
ATT_HEADS = 16
ATT_KV_HEADS = 2
ATT_HEAD_DIM = 64
ATT_GROUP = ATT_HEADS // ATT_KV_HEADS
WINDOW = 128
BLOCK = 128
REL_BUCKETS = 32
REL_MAX_DIST = 128
RET_HEADS = 4
RET_CHUNK = 128
ROPE_BASE = 10000.0

LANES = 128
SUBLANES = 8
VMEM_LIMIT_BYTES = 56 * 1024 * 1024

F32 = jnp.float32
BF16 = jnp.bfloat16


def _rms(x, g):
    ms = jnp.mean(x * x, axis=-1, keepdims=True)
    return x * lax.rsqrt(ms + NORM_EPS) * g


def _dot(a, b):
    return jnp.dot(a, b, preferred_element_type=F32)


def _dot_nt(a, b):
    return lax.dot_general(a, b, (((1,), (1,)), ((), ())), preferred_element_type=F32)


def _dot_tn(a, b):
    return lax.dot_general(a, b, (((0,), (0,)), ((), ())), preferred_element_type=F32)


def _resident(shape):
    nd = len(shape)
    return pl.BlockSpec(shape, lambda *_: (0,) * nd, pipeline_mode=pl.Buffered(1))


def _gain_spec_d(row, d):
    return pl.BlockSpec((None, 1, d), lambda *_: (row, 0, 0))


def _params(n_grid):
    return pltpu.CompilerParams(
        dimension_semantics=("arbitrary",) * n_grid,
        vmem_limit_bytes=VMEM_LIMIT_BYTES,
    )


def _ffn_kernel(x_ref, gpre_ref, gpost_ref, wgu_ref, wd_ref, o_ref, *, d_ff, n_chunks, n_sub):
    tm = x_ref.shape[0]
    ts = tm // n_sub
    fc = d_ff // n_chunks
    for sub in range(n_sub):
        rows = slice(sub * ts, (sub + 1) * ts)
        x = x_ref[rows, :]
        xn = _rms(x, gpre_ref[...]).astype(BF16)
        y = None
        for c in range(n_chunks):
            a = _dot(xn, wgu_ref[:, c * fc:(c + 1) * fc])
            b = _dot(xn, wgu_ref[:, d_ff + c * fc:d_ff + (c + 1) * fc])
            act = (jax.nn.silu(a) * b).astype(BF16)
            part = _dot(act, wd_ref[c * fc:(c + 1) * fc, :])
            y = part if y is None else y + part
        o_ref[rows, :] = x + 0.5 * _rms(y, gpost_ref[...])


def _ffn(x2, gains, row_pre, row_post, w_gu, w_down, *, tm, n_chunks, n_sub):
    t, d = x2.shape
    d_ff = w_down.shape[0]
    tile = pl.BlockSpec((tm, d), lambda i: (i, 0))
    return pl.pallas_call(
        functools.partial(_ffn_kernel, d_ff=d_ff, n_chunks=n_chunks, n_sub=n_sub),
        grid=(t // tm,),
        in_specs=[tile, _gain_spec_d(row_pre, d), _gain_spec_d(row_post, d),
                  _resident(w_gu.shape), _resident(w_down.shape)],
        out_specs=tile,
        out_shape=jax.ShapeDtypeStruct((t, d), F32),
        compiler_params=_params(1),
        name="ffn",
    )(x2, gains, gains, w_gu, w_down)


def _ple_kernel(x_ref, p_ref, g_ref, wg_ref, wp_ref, o_ref):
    x = x_ref[...]
    xn = _rms(x, g_ref[...]).astype(BF16)
    gate = jax.nn.sigmoid(_dot(xn, wg_ref[...]))
    proj = _dot(p_ref[...].astype(BF16), wp_ref[...])
    o_ref[...] = x + gate * proj


def _ple(x2, p3, layer, gains, row, w_gate, w_proj, *, tm):
    t, d = x2.shape
    pd = p3.shape[-1]
    tile = pl.BlockSpec((tm, d), lambda i: (i, 0))
    return pl.pallas_call(
        _ple_kernel,
        grid=(t // tm,),
        in_specs=[tile, pl.BlockSpec((None, tm, pd), lambda i: (layer, i, 0)),
                  _gain_spec_d(row, d), _resident(w_gate.shape), _resident(w_proj.shape)],
        out_specs=tile,
        out_shape=jax.ShapeDtypeStruct((t, d), F32),
        compiler_params=_params(1),
        name="ple",
    )(x2, p3, gains, w_gate, w_proj)


def _conv_kernel(x_ref, gpre_ref, gpost_ref, win_ref, cw_ref, wout_ref, o_ref, u_buf):
    tm, d = x_ref.shape
    x = x_ref[...]
    h = _rms(x, gpre_ref[...]).astype(BF16)
    proj = _dot(h, win_ref[...])
    bgate, cgate, v = proj[:, :d], proj[:, d:2 * d], proj[:, 2 * d:]
    u = cgate * v

    @pl.when(pl.program_id(1) == 0)
    def _():
        u_buf[0:SUBLANES, :] = jnp.zeros((SUBLANES, d), F32)

    u_buf[SUBLANES:, :] = u
    cw = cw_ref[...]
    conv = (cw[2:3, :] * u
            + cw[1:2, :] * u_buf[SUBLANES - 1:SUBLANES - 1 + tm, :]
            + cw[0:1, :] * u_buf[SUBLANES - 2:SUBLANES - 2 + tm, :])
    u_buf[0:SUBLANES, :] = u[tm - SUBLANES:, :]
    y = _dot((bgate * conv).astype(BF16), wout_ref[...])
    o_ref[...] = x + _rms(y, gpost_ref[...])


def _conv_mixer(x3, gains, row_pre, row_post, w_in, conv_w, w_out, *, tm):
    b, s, d = x3.shape
    tile = pl.BlockSpec((None, tm, d), lambda i, j: (i, j, 0))
    return pl.pallas_call(
        _conv_kernel,
        grid=(b, s // tm),
        in_specs=[tile, _gain_spec_d(row_pre, d), _gain_spec_d(row_post, d),
                  _resident(w_in.shape), _resident(conv_w.shape), _resident(w_out.shape)],
        out_specs=tile,
        out_shape=jax.ShapeDtypeStruct((b, s, d), F32),
        scratch_shapes=[pltpu.VMEM((tm + SUBLANES, d), F32)],
        compiler_params=_params(2),
        name="conv_mixer",
    )(x3, gains, gains, w_in, conv_w, w_out)


def _rel_bucket(dist):
    max_exact = REL_BUCKETS // 2
    dd = jnp.maximum(dist, 1).astype(F32)
    large = max_exact + (jnp.log(dd / max_exact) / math.log(REL_MAX_DIST / max_exact)
                         * (REL_BUCKETS - max_exact)).astype(jnp.int32)
    large = jnp.minimum(large, REL_BUCKETS - 1)
    return jnp.where(dist < max_exact, dist, large)


def _bias_kernel(rb_ref, bucket_ref, o_ref):
    bucket = bucket_ref[...]
    qi = lax.broadcasted_iota(jnp.int32, bucket.shape, 0)
    kj = lax.broadcasted_iota(jnp.int32, bucket.shape, 1)
    causal = kj <= qi
    hits = [bucket == bkt for bkt in range(REL_BUCKETS)]
    neg = jnp.full(bucket.shape, -jnp.inf, F32)
    for h in range(ATT_HEADS):
        acc = jnp.zeros(bucket.shape, F32)
        for bkt in range(REL_BUCKETS):
            acc = jnp.where(hits[bkt], rb_ref[bkt, h], acc)
        o_ref[0, h] = jnp.where(causal, acc, neg)
        o_ref[1, h] = acc


def _bias_tables(rel_bias):
    assert WINDOW == BLOCK
    qi = jnp.arange(BLOCK)[:, None]
    kj = jnp.arange(BLOCK)[None, :]
    dist = jnp.where(kj <= qi, qi - kj, qi + BLOCK - kj)
    bucket = _rel_bucket(dist).astype(jnp.int32)
    out = pl.pallas_call(
        _bias_kernel,
        in_specs=[pl.BlockSpec(memory_space=pltpu.SMEM),
                  pl.BlockSpec(memory_space=pltpu.VMEM)],
        out_specs=pl.BlockSpec(memory_space=pltpu.VMEM),
        out_shape=jax.ShapeDtypeStruct((2, ATT_HEADS, BLOCK, BLOCK), F32),
        name="rel_bias_table",
    )(rel_bias, bucket)
    return out.reshape(2, ATT_KV_HEADS, ATT_GROUP * BLOCK, BLOCK)


def _swa_kernel(x_ref, xprev_ref, gpre_ref, gpost_ref, wqkv_ref, bias_ref, sink_ref, wo_ref, o_ref,
                kk_buf, vv_buf, att_buf):
    tm, d = x_ref.shape
    n_blk = tm // BLOCK
    dq = ATT_HEADS * ATT_HEAD_DIM
    dkv = ATT_KV_HEADS * ATT_HEAD_DIM
    first_tile = pl.program_id(1) == 0

    x = x_ref[...]
    h = _rms(x, gpre_ref[...]).astype(BF16)
    qkv = _dot(h, wqkv_ref[...])
    q = (qkv[:, :dq] * (ATT_HEAD_DIM ** -0.5)).astype(BF16)
    h_prev = _rms(xprev_ref[...], gpre_ref[...]).astype(BF16)
    kv_prev = _dot(h_prev, wqkv_ref[:, dq:])
    k = jnp.concatenate([kv_prev[:, :dkv], qkv[:, dq:dq + dkv]], axis=0)
    v = jnp.concatenate([kv_prev[:, dkv:], qkv[:, dq + dkv:]], axis=0)

    low = lax.broadcasted_iota(jnp.int32, k.shape, 1) < ATT_HEAD_DIM
    k_sw = pltpu.roll(k, ATT_HEAD_DIM, 1)
    v_sw = pltpu.roll(v, ATT_HEAD_DIM, 1)
    kk_buf[0] = jnp.where(low, k, k_sw).astype(BF16)
    kk_buf[1] = jnp.where(low, k_sw, k).astype(BF16)
    vv_buf[0, :, 0:LANES] = jnp.where(low, v, v_sw).astype(BF16)
    vv_buf[1, :, 0:LANES] = jnp.where(low, v_sw, v).astype(BF16)
    vv_buf[:, :, LANES:] = jnp.ones((ATT_KV_HEADS, tm + BLOCK, LANES), BF16)

    low_b = lax.broadcasted_iota(jnp.int32, (BLOCK, LANES), 1) < ATT_HEAD_DIM
    zero = jnp.zeros((BLOCK, LANES), BF16)
    pairs = ATT_GROUP // 2
    stacked = (2 * BLOCK, BLOCK)
    q_pos = lax.broadcasted_iota(jnp.int32, stacked, 0) & (BLOCK - 1)
    causal = lax.broadcasted_iota(jnp.int32, stacked, 1) <= q_pos
    zero_p = jnp.zeros(stacked, BF16)
    for n in range(n_blk):
        rows = slice(n * BLOCK, (n + 1) * BLOCK)
        variant = jnp.where(first_tile, 0, 1) if n == 0 else 1
        for hk in range(ATT_KV_HEADS):
            kb = kk_buf[hk, n * BLOCK:(n + 2) * BLOCK, :]
            vb = vv_buf[hk, n * BLOCK:(n + 2) * BLOCK, :]
            for j in range(pairs):
                col = (hk * pairs + j) * LANES
                pair_rows = slice(2 * j * BLOCK, (2 * j + 2) * BLOCK)
                q2 = q[rows, col:col + LANES]
                qs = jnp.concatenate([jnp.where(low_b, q2, zero), jnp.where(low_b, zero, q2)], axis=0)
                s2 = _dot_nt(qs, kb)
                logits = (jnp.where(causal, s2[:, BLOCK:], s2[:, :BLOCK])
                          + bias_ref[variant, hk, pair_rows, :])
                sink = sink_ref[hk, pair_rows, :]
                m = jnp.maximum(jnp.max(logits, axis=-1, keepdims=True), sink)
                e = jnp.exp(logits - m).astype(BF16)
                e2 = jnp.concatenate([jnp.where(causal, zero_p, e),
                                      jnp.where(causal, e, zero_p)], axis=1)
                r = _dot(e2, vb)
                out = r[:, :LANES] / (r[:, LANES:] + jnp.exp(sink - m))
                att_buf[rows, col:col + LANES] = jnp.where(low_b, out[:BLOCK], out[BLOCK:]).astype(BF16)

    y = _dot(att_buf[...], wo_ref[...])
    o_ref[...] = x + _rms(y, gpost_ref[...])


def _swa_mixer(x3, gains, row_pre, row_post, w_qkv, bias_tab, sinks, w_o, *, tm):
    b, s, d = x3.shape
    blocks_per_tile = tm // BLOCK
    tile = pl.BlockSpec((None, tm, d), lambda i, j: (i, j, 0))
    prev_block = pl.BlockSpec((None, BLOCK, d),
                              lambda i, j: (i, jnp.maximum(j * blocks_per_tile - 1, 0), 0))
    sink_col = jnp.repeat(sinks.astype(F32).reshape(ATT_KV_HEADS, ATT_GROUP), BLOCK, axis=1)
    sink_col = jnp.broadcast_to(sink_col[..., None], sink_col.shape + (LANES,))
    return pl.pallas_call(
        _swa_kernel,
        grid=(b, s // tm),
        in_specs=[tile, prev_block, _gain_spec_d(row_pre, d), _gain_spec_d(row_post, d),
                  _resident(w_qkv.shape), _resident(bias_tab.shape), _resident(sink_col.shape),
                  _resident(w_o.shape)],
        out_specs=tile,
        out_shape=jax.ShapeDtypeStruct((b, s, d), F32),
        scratch_shapes=[pltpu.VMEM((ATT_KV_HEADS, tm + BLOCK, LANES), BF16),
                        pltpu.VMEM((ATT_KV_HEADS, tm + BLOCK, 2 * LANES), BF16),
                        pltpu.VMEM((tm, ATT_HEADS * ATT_HEAD_DIM), BF16)],
        compiler_params=_params(2),
        name="swa_mixer",
    )(x3, x3, gains, gains, w_qkv, bias_tab, sink_col, w_o)


def _ret_tables(s, dk):
    f32 = F32
    half = dk // 2
    inv = ROPE_BASE ** (-jnp.arange(half, dtype=f32) / half)
    ang = jnp.arange(s)[:, None].astype(f32) * inv[None, :]
    c = RET_CHUNK
    log_g = jnp.log(1.0 - 2.0 ** (-5.0 - jnp.arange(RET_HEADS, dtype=f32)))
    idx = jnp.arange(c, dtype=f32)
    diff = idx[:, None] - idx[None, :]
    decay_mask = jnp.where(diff >= 0, jnp.exp(log_g[:, None, None] * jnp.maximum(diff, 0.0)), 0.0)
    q_decay = jnp.exp(log_g[:, None] * (idx + 1.0))[..., None]
    k_decay = jnp.exp(log_g[:, None] * (c - 1.0 - idx))[..., None]
    chunk_decay = jnp.exp(log_g * c)
    return jnp.cos(ang), jnp.sin(ang), decay_mask, q_decay, k_decay, chunk_decay


def _ret_kernel(x_ref, gpre_ref, gpost_ref, w_ref, cos_ref, sin_ref, dmask_ref, qdec_ref,
                kdec_ref, cdec_ref, wo_ref, o_ref, state_ref, gated_buf, *, dk, dv):
    tm, d = x_ref.shape
    n_chunk = tm // RET_CHUNK
    half = dk // 2
    hq = RET_HEADS * dk

    @pl.when(pl.program_id(1) == 0)
    def _():
        state_ref[...] = jnp.zeros(state_ref.shape, F32)

    x = x_ref[...]
    h = _rms(x, gpre_ref[...]).astype(BF16)
    proj = _dot(h, w_ref[...])
    cos = cos_ref[...]
    sin = sin_ref[...]

    def rot(t):
        t1, t2 = t[:, :half], t[:, half:]
        return jnp.concatenate([t1 * cos - t2 * sin, t1 * sin + t2 * cos], axis=-1)

    for hd in range(RET_HEADS):
        q = rot(proj[:, hd * dk:(hd + 1) * dk])
        k = rot(proj[:, hq + hd * dk:hq + (hd + 1) * dk]) * (dk ** -0.5)
        v = proj[:, 2 * hq + hd * dv:2 * hq + (hd + 1) * dv]
        g = proj[:, 2 * hq + RET_HEADS * dv + hd * dv:2 * hq + RET_HEADS * dv + (hd + 1) * dv]
        dmask = dmask_ref[hd]
        qdec = qdec_ref[hd]
        kdec = kdec_ref[hd]
        cdec = cdec_ref[hd]
        state = state_ref[hd]
        for c in range(n_chunk):
            rows = slice(c * RET_CHUNK, (c + 1) * RET_CHUNK)
            qc = q[rows].astype(BF16)
            kc = k[rows]
            vc = v[rows].astype(BF16)
            inner = _dot_nt(qc, kc.astype(BF16)) * dmask
            o = _dot(inner.astype(BF16), vc) + _dot(qc, state.astype(BF16)) * qdec
            state = state * cdec + _dot_tn((kc * kdec).astype(BF16), vc)
            mu = jnp.mean(o, axis=-1, keepdims=True)
            oc = o - mu
            var = jnp.mean(oc * oc, axis=-1, keepdims=True)
            on = oc * lax.rsqrt(var + NORM_EPS)
            gated_buf[rows, hd * dv:(hd + 1) * dv] = (jax.nn.silu(g[rows]) * on).astype(BF16)
        state_ref[hd] = state

    y = _dot(gated_buf[...], wo_ref[...])
    o_ref[...] = x + _rms(y, gpost_ref[...])


def _ret_mixer(x3, gains, row_pre, row_post, w_qkvg, w_o, *, tm):
    b, s, d = x3.shape
    dk = d // RET_HEADS
    dv = 2 * dk
    cos, sin, dmask, qdec, kdec, cdec = _ret_tables(s, dk)
    cdec = cdec.reshape(RET_HEADS, 1, 1)
    tile = pl.BlockSpec((None, tm, d), lambda i, j: (i, j, 0))
    rope = pl.BlockSpec((tm, dk // 2), lambda i, j: (j, 0))
    return pl.pallas_call(
        functools.partial(_ret_kernel, dk=dk, dv=dv),
        grid=(b, s // tm),
        in_specs=[tile, _gain_spec_d(row_pre, d), _gain_spec_d(row_post, d),
                  _resident(w_qkvg.shape), rope, rope, _resident(dmask.shape),
                  _resident(qdec.shape), _resident(kdec.shape), _resident(cdec.shape),
                  _resident(w_o.shape)],
        out_specs=tile,
        out_shape=jax.ShapeDtypeStruct((b, s, d), F32),
        scratch_shapes=[pltpu.VMEM((RET_HEADS, dk, dv), F32),
                        pltpu.VMEM((tm, RET_HEADS * dv), BF16)],
        compiler_params=_params(2),
        name="ret_mixer",
    )(x3, gains, gains, w_qkvg, cos, sin, dmask, qdec, kdec, cdec, w_o)


def _tile(n, want):
    t = min(n, want)
    assert n % t == 0, (n, t)
    return t


def kernel(x, p, norm_g, ffn_w_gu, ffn_w_down, ple_w_proj, ple_w_gate, rel_bias, conv_w_in, conv_w,
           conv_w_out, swa_w_qkv, swa_sinks, swa_w_o, ret_w_qkvg, ret_w_o):
    depth = norm_g.shape[0]
    b, s, d = x.shape
    t = b * s
    gains = norm_g.astype(F32).reshape(depth * N_NORMS, 1, d)
    p3 = p.reshape(depth, t, p.shape[-1])
    bf = lambda w: w.astype(BF16)
    bias_tab = _bias_tables(rel_bias.astype(F32)) if depth > 1 else None

    tm_ffn = _tile(t, 512)
    tm_ple = _tile(t, 1024)
    tm_conv = _tile(s, 512)
    tm_swa = _tile(s, 512)
    tm_ret = _tile(s, 512)

    for i in range(depth):
        kind, j = i % N_MIXERS, i // N_MIXERS
        r = i * N_NORMS
        x2 = _ffn(x.reshape(t, d), gains, r + 0, r + 1, bf(ffn_w_gu[i, 0]), bf(ffn_w_down[i, 0]),
                  tm=tm_ffn, n_chunks=1, n_sub=2)
        x = x2.reshape(b, s, d)
        if kind == 0:
            x = _conv_mixer(x, gains, r + 2, r + 3, bf(conv_w_in[j]), conv_w[j].astype(F32),
                            bf(conv_w_out[j]), tm=tm_conv)
        elif kind == 1:
            x = _swa_mixer(x, gains, r + 2, r + 3, bf(swa_w_qkv[j]), bias_tab, swa_sinks[j],
                           bf(swa_w_o[j]), tm=tm_swa)
        else:
            x = _ret_mixer(x, gains, r + 2, r + 3, bf(ret_w_qkvg[j]), bf(ret_w_o[j]), tm=tm_ret)
        x2 = _ffn(x.reshape(t, d), gains, r + 4, r + 5, bf(ffn_w_gu[i, 1]), bf(ffn_w_down[i, 1]),
                  tm=tm_ffn, n_chunks=1, n_sub=2)
        x2 = _ple(x2, p3, i, gains, r + 6, bf(ple_w_gate[i]), bf(ple_w_proj[i]), tm=tm_ple)
        x = x2.reshape(b, s, d)
    return x
```

```python
import functools
import math

import jax
import jax.numpy as jnp
from jax import lax
from jax.experimental import pallas as pl
from jax.experimental.pallas import tpu as pltpu

N_MIXERS = 3
N_NORMS = 7
NORM_EPS = 1e-6
CONV_WIDTH = 3
ATT_HEADS = 16
ATT_KV_HEADS = 2
ATT_HEAD_DIM = 64
ATT_GROUP = ATT_HEADS // ATT_KV_HEADS
WINDOW = 128
BLOCK = 128
REL_BUCKETS = 32
REL_MAX_DIST = 128
RET_HEADS = 4
RET_CHUNK = 256
ROPE_BASE = 10000.0

LANES = 128
SUBLANES = 8
VMEM_LIMIT_BYTES = 56 * 1024 * 1024

F32 = jnp.float32
BF16 = jnp.bfloat16


def _rms(x, g):
    ms = jnp.mean(x * x, axis=-1, keepdims=True)
    return x * lax.rsqrt(ms + NORM_EPS) * g


def _dot(a, b):
    return jnp.dot(a, b, preferred_element_type=F32)


def _dot_nt(a, b):
    return lax.dot_general(a, b, (((1,), (1,)), ((), ())), preferred_element_type=F32)


def _dot_tn(a, b):
    return lax.dot_general(a, b, (((0,), (0,)), ((), ())), preferred_element_type=F32)


def _resident(shape):
    nd = len(shape)
    return pl.BlockSpec(shape, lambda *_: (0,) * nd, pipeline_mode=pl.Buffered(1))


def _gain_spec_d(row, d):
    return pl.BlockSpec((None, 1, d), lambda *_: (row, 0, 0))


def _params(n_grid):
    return pltpu.CompilerParams(
        dimension_semantics=("arbitrary",) * n_grid,
        vmem_limit_bytes=VMEM_LIMIT_BYTES,
    )


def _ffn_kernel(x_ref, gpre_ref, gpost_ref, wgu_ref, wd_ref, o_ref, *, d_ff, n_chunks, n_sub):
    tm = x_ref.shape[0]
    ts = tm // n_sub
    fc = d_ff // n_chunks
    for sub in range(n_sub):
        rows = slice(sub * ts, (sub + 1) * ts)
        x = x_ref[rows, :]
        xn = _rms(x, gpre_ref[...]).astype(BF16)
        y = None
        for c in range(n_chunks):
            a = _dot(xn, wgu_ref[:, c * fc:(c + 1) * fc])
            b = _dot(xn, wgu_ref[:, d_ff + c * fc:d_ff + (c + 1) * fc])
            act = (jax.nn.silu(a) * b).astype(BF16)
            part = _dot(act, wd_ref[c * fc:(c + 1) * fc, :])
            y = part if y is None else y + part
        o_ref[rows, :] = x + 0.5 * _rms(y, gpost_ref[...])


def _ffn(x2, gains, row_pre, row_post, w_gu, w_down, *, tm, n_chunks, n_sub):
    t, d = x2.shape
    d_ff = w_down.shape[0]
    tile = pl.BlockSpec((tm, d), lambda i: (i, 0))
    return pl.pallas_call(
        functools.partial(_ffn_kernel, d_ff=d_ff, n_chunks=n_chunks, n_sub=n_sub),
        grid=(t // tm,),
        in_specs=[tile, _gain_spec_d(row_pre, d), _gain_spec_d(row_post, d),
                  _resident(w_gu.shape), _resident(w_down.shape)],
        out_specs=tile,
        out_shape=jax.ShapeDtypeStruct((t, d), F32),
        compiler_params=_params(1),
        name="ffn",
    )(x2, gains, gains, w_gu, w_down)


def _ple_kernel(x_ref, p_ref, g_ref, wg_ref, wp_ref, o_ref):
    x = x_ref[...]
    xn = _rms(x, g_ref[...]).astype(BF16)
    gate = jax.nn.sigmoid(_dot(xn, wg_ref[...]))
    proj = _dot(p_ref[...].astype(BF16), wp_ref[...])
    o_ref[...] = x + gate * proj


def _ple(x2, p3, layer, gains, row, w_gate, w_proj, *, tm):
    t, d = x2.shape
    pd = p3.shape[-1]
    tile = pl.BlockSpec((tm, d), lambda i: (i, 0))
    return pl.pallas_call(
        _ple_kernel,
        grid=(t // tm,),
        in_specs=[tile, pl.BlockSpec((None, tm, pd), lambda i: (layer, i, 0)),
                  _gain_spec_d(row, d), _resident(w_gate.shape), _resident(w_proj.shape)],
        out_specs=tile,
        out_shape=jax.ShapeDtypeStruct((t, d), F32),
        compiler_params=_params(1),
        name="ple",
    )(x2, p3, gains, w_gate, w_proj)


def _conv_kernel(x_ref, gpre_ref, gpost_ref, win_ref, cw_ref, wout_ref, o_ref, u_buf, *, n_sub):
    tm, d = x_ref.shape
    ts = tm // n_sub

    @pl.when(pl.program_id(1) == 0)
    def _():
        u_buf[0:SUBLANES, :] = jnp.zeros((SUBLANES, d), F32)

    cw = cw_ref[...]
    u = None
    for sub in range(n_sub):
        r0 = sub * ts
        x = x_ref[r0:r0 + ts, :]
        h = _rms(x, gpre_ref[...]).astype(BF16)
        proj = _dot(h, win_ref[...])
        bgate, cgate, v = proj[:, :d], proj[:, d:2 * d], proj[:, 2 * d:]
        u = cgate * v
        u_buf[SUBLANES + r0:SUBLANES + r0 + ts, :] = u
        conv = (cw[2:3, :] * u
                + cw[1:2, :] * u_buf[SUBLANES - 1 + r0:SUBLANES - 1 + r0 + ts, :]
                + cw[0:1, :] * u_buf[SUBLANES - 2 + r0:SUBLANES - 2 + r0 + ts, :])
        y = _dot((bgate * conv).astype(BF16), wout_ref[...])
        o_ref[r0:r0 + ts, :] = x + _rms(y, gpost_ref[...])
    u_buf[0:SUBLANES, :] = u[ts - SUBLANES:, :]


def _conv_mixer(x3, gains, row_pre, row_post, w_in, conv_w, w_out, *, tm, n_sub):
    b, s, d = x3.shape
    tile = pl.BlockSpec((None, tm, d), lambda i, j: (i, j, 0))
    return pl.pallas_call(
        functools.partial(_conv_kernel, n_sub=n_sub),
        grid=(b, s // tm),
        in_specs=[tile, _gain_spec_d(row_pre, d), _gain_spec_d(row_post, d),
                  _resident(w_in.shape), _resident(conv_w.shape), _resident(w_out.shape)],
        out_specs=tile,
        out_shape=jax.ShapeDtypeStruct((b, s, d), F32),
        scratch_shapes=[pltpu.VMEM((tm + SUBLANES, d), F32)],
        compiler_params=_params(2),
        name="conv_mixer",
    )(x3, gains, gains, w_in, conv_w, w_out)


def _rel_bucket(dist):
    max_exact = REL_BUCKETS // 2
    dd = jnp.maximum(dist, 1).astype(F32)
    large = max_exact + (jnp.log(dd / max_exact) / math.log(REL_MAX_DIST / max_exact)
                         * (REL_BUCKETS - max_exact)).astype(jnp.int32)
    large = jnp.minimum(large, REL_BUCKETS - 1)
    return jnp.where(dist < max_exact, dist, large)


def _bias_kernel(rb_ref, bucket_ref, o_ref):
    bucket = bucket_ref[...]
    qi = lax.broadcasted_iota(jnp.int32, bucket.shape, 0)
    kj = lax.broadcasted_iota(jnp.int32, bucket.shape, 1)
    causal = kj <= qi
    hits = [bucket == bkt for bkt in range(REL_BUCKETS)]
    neg = jnp.full(bucket.shape, -jnp.inf, F32)
    for h in range(ATT_HEADS):
        acc = jnp.zeros(bucket.shape, F32)
        for bkt in range(REL_BUCKETS):
            acc = jnp.where(hits[bkt], rb_ref[bkt, h], acc)
        o_ref[0, h] = jnp.where(causal, acc, neg)
        o_ref[1, h] = acc


def _bias_tables(rel_bias):
    assert WINDOW == BLOCK
    qi = jnp.arange(BLOCK)[:, None]
    kj = jnp.arange(BLOCK)[None, :]
    dist = jnp.where(kj <= qi, qi - kj, qi + BLOCK - kj)
    bucket = _rel_bucket(dist).astype(jnp.int32)
    out = pl.pallas_call(
        _bias_kernel,
        in_specs=[pl.BlockSpec(memory_space=pltpu.SMEM),
                  pl.BlockSpec(memory_space=pltpu.VMEM)],
        out_specs=pl.BlockSpec(memory_space=pltpu.VMEM),
        out_shape=jax.ShapeDtypeStruct((2, ATT_HEADS, BLOCK, BLOCK), F32),
        name="rel_bias_table",
    )(rel_bias, bucket)
    return out.reshape(2, ATT_KV_HEADS, ATT_GROUP * BLOCK, BLOCK)


def _swa_kernel(x_ref, xprev_ref, gpre_ref, gpost_ref, wqkv_ref, bias_ref, sink_ref, wo_ref, o_ref,
                kk_buf, vv_buf, att_buf):
    tm, d = x_ref.shape
    n_blk = tm // BLOCK
    dq = ATT_HEADS * ATT_HEAD_DIM
    dkv = ATT_KV_HEADS * ATT_HEAD_DIM
    first_tile = pl.program_id(1) == 0

    x = x_ref[...]
    h = _rms(x, gpre_ref[...]).astype(BF16)
    qkv = _dot(h, wqkv_ref[...])
    q = (qkv[:, :dq] * (ATT_HEAD_DIM ** -0.5)).astype(BF16)
    h_prev = _rms(xprev_ref[...], gpre_ref[...]).astype(BF16)
    kv_prev = _dot(h_prev, wqkv_ref[:, dq:])
    k = jnp.concatenate([kv_prev[:, :dkv], qkv[:, dq:dq + dkv]], axis=0)
    v = jnp.concatenate([kv_prev[:, dkv:], qkv[:, dq + dkv:]], axis=0)

    low = lax.broadcasted_iota(jnp.int32, k.shape, 1) < ATT_HEAD_DIM
    k_sw = pltpu.roll(k, ATT_HEAD_DIM, 1)
    v_sw = pltpu.roll(v, ATT_HEAD_DIM, 1)
    kk_buf[0] = jnp.where(low, k, k_sw).astype(BF16)
    kk_buf[1] = jnp.where(low, k_sw, k).astype(BF16)
    vv_buf[0, :, 0:LANES] = jnp.where(low, v, v_sw).astype(BF16)
    vv_buf[1, :, 0:LANES] = jnp.where(low, v_sw, v).astype(BF16)
    vv_buf[:, :, LANES:] = jnp.ones((ATT_KV_HEADS, tm + BLOCK, LANES), BF16)

    low_b = lax.broadcasted_iota(jnp.int32, (BLOCK, LANES), 1) < ATT_HEAD_DIM
    zero = jnp.zeros((BLOCK, LANES), BF16)
    pairs = ATT_GROUP // 2
    stacked = (2 * BLOCK, BLOCK)
    q_pos = lax.broadcasted_iota(jnp.int32, stacked, 0) & (BLOCK - 1)
    causal = lax.broadcasted_iota(jnp.int32, stacked, 1) <= q_pos
    zero_p = jnp.zeros(stacked, BF16)
    for n in range(n_blk):
        rows = slice(n * BLOCK, (n + 1) * BLOCK)
        variant = jnp.where(first_tile, 0, 1) if n == 0 else 1
        for hk in range(ATT_KV_HEADS):
            kb = kk_buf[hk, n * BLOCK:(n + 2) * BLOCK, :]
            vb = vv_buf[hk, n * BLOCK:(n + 2) * BLOCK, :]
            for j in range(pairs):
                col = (hk * pairs + j) * LANES
                pair_rows = slice(2 * j * BLOCK, (2 * j + 2) * BLOCK)
                q2 = q[rows, col:col + LANES]
                qs = jnp.concatenate([jnp.where(low_b, q2, zero), jnp.where(low_b, zero, q2)], axis=0)
                s2 = _dot_nt(qs, kb)
                logits = (jnp.where(causal, s2[:, BLOCK:], s2[:, :BLOCK])
                          + bias_ref[variant, hk, pair_rows, :])
                sink = sink_ref[hk, pair_rows, :]
                m = jnp.maximum(jnp.max(logits, axis=-1, keepdims=True), sink)
                e = jnp.exp(logits - m).astype(BF16)
                e2 = jnp.concatenate([jnp.where(causal, zero_p, e),
                                      jnp.where(causal, e, zero_p)], axis=1)
                r = _dot(e2, vb)
                out = r[:, :LANES] / (r[:, LANES:] + jnp.exp(sink - m))
                att_buf[rows, col:col + LANES] = jnp.where(low_b, out[:BLOCK], out[BLOCK:]).astype(BF16)

    y = _dot(att_buf[...], wo_ref[...])
    o_ref[...] = x + _rms(y, gpost_ref[...])


def _swa_mixer(x3, gains, row_pre, row_post, w_qkv, bias_tab, sinks, w_o, *, tm):
    b, s, d = x3.shape
    blocks_per_tile = tm // BLOCK
    tile = pl.BlockSpec((None, tm, d), lambda i, j: (i, j, 0))
    prev_block = pl.BlockSpec((None, BLOCK, d),
                              lambda i, j: (i, jnp.maximum(j * blocks_per_tile - 1, 0), 0))
    sink_col = jnp.repeat(sinks.astype(F32).reshape(ATT_KV_HEADS, ATT_GROUP), BLOCK, axis=1)
    sink_col = jnp.broadcast_to(sink_col[..., None], sink_col.shape + (LANES,))
    return pl.pallas_call(
        _swa_kernel,
        grid=(b, s // tm),
        in_specs=[tile, prev_block, _gain_spec_d(row_pre, d), _gain_spec_d(row_post, d),
                  _resident(w_qkv.shape), _resident(bias_tab.shape), _resident(sink_col.shape),
                  _resident(w_o.shape)],
        out_specs=tile,
        out_shape=jax.ShapeDtypeStruct((b, s, d), F32),
        scratch_shapes=[pltpu.VMEM((ATT_KV_HEADS, tm + BLOCK, LANES), BF16),
                        pltpu.VMEM((ATT_KV_HEADS, tm + BLOCK, 2 * LANES), BF16),
                        pltpu.VMEM((tm, ATT_HEADS * ATT_HEAD_DIM), BF16)],
        compiler_params=_params(2),
        name="swa_mixer",
    )(x3, x3, gains, gains, w_qkv, bias_tab, sink_col, w_o)


def _ret_tables(s, dk):
    f32 = F32
    half = dk // 2
    inv = ROPE_BASE ** (-jnp.arange(half, dtype=f32) / half)
    ang = jnp.arange(s)[:, None].astype(f32) * inv[None, :]
    c = RET_CHUNK
    log_g = jnp.log(1.0 - 2.0 ** (-5.0 - jnp.arange(RET_HEADS, dtype=f32)))
    idx = jnp.arange(c, dtype=f32)
    diff = idx[:, None] - idx[None, :]
    decay_mask = jnp.where(diff >= 0, jnp.exp(log_g[:, None, None] * jnp.maximum(diff, 0.0)), 0.0)
    q_decay = jnp.exp(log_g[:, None] * (idx + 1.0))[..., None]
    k_decay = jnp.exp(log_g[:, None] * (c - 1.0 - idx))[..., None]
    chunk_decay = jnp.exp(log_g * c)
    return jnp.cos(ang), jnp.sin(ang), decay_mask, q_decay, k_decay, chunk_decay


def _ret_kernel(x_ref, gpre_ref, gpost_ref, w_ref, cos_ref, sin_ref, dmask_ref, qdec_ref,
                kdec_ref, cdec_ref, wo_ref, o_ref, state_ref, gated_buf, *, dk, dv, n_sub):
    tm, d = x_ref.shape
    ts = tm // n_sub
    n_chunk = ts // RET_CHUNK
    half = dk // 2
    hq = RET_HEADS * dk

    @pl.when(pl.program_id(1) == 0)
    def _():
        state_ref[...] = jnp.zeros(state_ref.shape, F32)

    for sub in range(n_sub):
        r0 = sub * ts
        x = x_ref[r0:r0 + ts, :]
        h = _rms(x, gpre_ref[...]).astype(BF16)
        proj = _dot(h, w_ref[...])
        cos = cos_ref[r0:r0 + ts, :]
        sin = sin_ref[r0:r0 + ts, :]

        def rot(t):
            t1, t2 = t[:, :half], t[:, half:]
            return jnp.concatenate([t1 * cos - t2 * sin, t1 * sin + t2 * cos], axis=-1)

        for hd in range(RET_HEADS):
            q = rot(proj[:, hd * dk:(hd + 1) * dk])
            k = rot(proj[:, hq + hd * dk:hq + (hd + 1) * dk]) * (dk ** -0.5)
            v = proj[:, 2 * hq + hd * dv:2 * hq + (hd + 1) * dv]
            g = proj[:, 2 * hq + RET_HEADS * dv + hd * dv:2 * hq + RET_HEADS * dv + (hd + 1) * dv]
            dmask = dmask_ref[hd]
            qdec = qdec_ref[hd]
            kdec = kdec_ref[hd]
            cdec = cdec_ref[hd]
            state = state_ref[hd]
            for c in range(n_chunk):
                rows = slice(c * RET_CHUNK, (c + 1) * RET_CHUNK)
                qc = q[rows].astype(BF16)
                kc = k[rows]
                vc = v[rows].astype(BF16)
                inner = _dot_nt(qc, kc.astype(BF16)) * dmask
                o = _dot(inner.astype(BF16), vc) + _dot(qc, state.astype(BF16)) * qdec
                state = state * cdec + _dot_tn((kc * kdec).astype(BF16), vc)
                mu = jnp.mean(o, axis=-1, keepdims=True)
                oc = o - mu
                var = jnp.mean(oc * oc, axis=-1, keepdims=True)
                on = oc * lax.rsqrt(var + NORM_EPS)
                gated_buf[r0 + c * RET_CHUNK:r0 + (c + 1) * RET_CHUNK, hd * dv:(hd + 1) * dv] = (
                    jax.nn.silu(g[rows]) * on).astype(BF16)
            state_ref[hd] = state

        y = _dot(gated_buf[r0:r0 + ts, :], wo_ref[...])
        o_ref[r0:r0 + ts, :] = x + _rms(y, gpost_ref[...])


def _ret_mixer(x3, gains, row_pre, row_post, w_qkvg, w_o, *, tm, n_sub):
    b, s, d = x3.shape
    dk = d // RET_HEADS
    dv = 2 * dk
    cos, sin, dmask, qdec, kdec, cdec = _ret_tables(s, dk)
    cdec = cdec.reshape(RET_HEADS, 1, 1)
    tile = pl.BlockSpec((None, tm, d), lambda i, j: (i, j, 0))
    rope = pl.BlockSpec((tm, dk // 2), lambda i, j: (j, 0))
    return pl.pallas_call(
        functools.partial(_ret_kernel, dk=dk, dv=dv, n_sub=n_sub),
        grid=(b, s // tm),
        in_specs=[tile, _gain_spec_d(row_pre, d), _gain_spec_d(row_post, d),
                  _resident(w_qkvg.shape), rope, rope, _resident(dmask.shape),
                  _resident(qdec.shape), _resident(kdec.shape), _resident(cdec.shape),
                  _resident(w_o.shape)],
        out_specs=tile,
        out_shape=jax.ShapeDtypeStruct((b, s, d), F32),
        scratch_shapes=[pltpu.VMEM((RET_HEADS, dk, dv), F32),
                        pltpu.VMEM((tm, RET_HEADS * dv), BF16)],
        compiler_params=_params(2),
        name="ret_mixer",
    )(x3, gains, gains, w_qkvg, cos, sin, dmask, qdec, kdec, cdec, w_o)


def _tile(n, want):
    t = min(n, want)
    assert n % t == 0, (n, t)
    return t


def kernel(x, p, norm_g, ffn_w_gu, ffn_w_down, ple_w_proj, ple_w_gate, rel_bias, conv_w_in, conv_w,
           conv_w_out, swa_w_qkv, swa_sinks, swa_w_o, ret_w_qkvg, ret_w_o):
    depth = norm_g.shape[0]
    b, s, d = x.shape
    t = b * s
    gains = norm_g.astype(F32).reshape(depth * N_NORMS, 1, d)
    p3 = p.reshape(depth, t, p.shape[-1])
    bf = lambda w: w.astype(BF16)
    bias_tab = _bias_tables(rel_bias.astype(F32)) if depth > 1 else None

    tm_ffn = _tile(t, 1024)
    tm_ple = _tile(t, 1024)
    tm_conv = _tile(s, 512)
    tm_swa = _tile(s, 512)
    tm_ret = _tile(s, 512)

    for i in range(depth):
        kind, j = i % N_MIXERS, i // N_MIXERS
        r = i * N_NORMS
        x2 = _ffn(x.reshape(t, d), gains, r + 0, r + 1, bf(ffn_w_gu[i, 0]), bf(ffn_w_down[i, 0]),
                  tm=tm_ffn, n_chunks=1, n_sub=4)
        x = x2.reshape(b, s, d)
        if kind == 0:
            x = _conv_mixer(x, gains, r + 2, r + 3, bf(conv_w_in[j]), conv_w[j].astype(F32),
                            bf(conv_w_out[j]), tm=tm_conv, n_sub=2)
        elif kind == 1:
            x = _swa_mixer(x, gains, r + 2, r + 3, bf(swa_w_qkv[j]), bias_tab, swa_sinks[j],
                           bf(swa_w_o[j]), tm=tm_swa)
        else:
            x = _ret_mixer(x, gains, r + 2, r + 3, bf(ret_w_qkvg[j]), bf(ret_w_o[j]), tm=tm_ret,
                           n_sub=2)
        x2 = _ffn(x.reshape(t, d), gains, r + 4, r + 5, bf(ffn_w_gu[i, 1]), bf(ffn_w_down[i, 1]),
                  tm=tm_ffn, n_chunks=1, n_sub=4)
        x2 = _ple(x2, p3, i, gains, r + 6, bf(ple_w_gate[i]), bf(ple_w_proj[i]), tm=tm_ple)
        x = x2.reshape(b, s, d)
    return x
```

```python
import functools
import math

import jax
import jax.numpy as jnp
from jax import lax
from jax.experimental import pallas as pl
from jax.experimental.pallas import tpu as pltpu

N_MIXERS = 3
N_NORMS = 7
NORM_EPS = 1e-6
CONV_WIDTH = 3
ATT_HEADS = 16
ATT_KV_HEADS = 2
ATT_HEAD_DIM = 64
ATT_GROUP = ATT_HEADS // ATT_KV_HEADS
WINDOW = 128
BLOCK = 128
REL_BUCKETS = 32
REL_MAX_DIST = 128
RET_HEADS = 4
RET_CHUNK = 256
ROPE_BASE = 10000.0

LANES = 128
SUBLANES = 8
VMEM_LIMIT_BYTES = 56 * 1024 * 1024
STAGE_SLOT_ELEMS = 384 * 1024

F32 = jnp.float32
BF16 = jnp.bfloat16


def _rms(x, g):
    ms = jnp.mean(x * x, axis=-1, keepdims=True)
    return x * lax.rsqrt(ms + NORM_EPS) * g


def _dot(a, b):
    return jnp.dot(a, b, preferred_element_type=F32)


def _dot_nt(a, b):
    return lax.dot_general(a, b, (((1,), (1,)), ((), ())), preferred_element_type=F32)


def _dot_tn(a, b):
    return lax.dot_general(a, b, (((0,), (0,)), ((), ())), preferred_element_type=F32)


def _resident(shape):
    nd = len(shape)
    return pl.BlockSpec(shape, lambda *_: (0,) * nd, pipeline_mode=pl.Buffered(1))


def _stage_chunk_rows(n_rows, n_cols):
    rows = 1 << ((STAGE_SLOT_ELEMS // n_cols).bit_length() - 1)
    rows = min(rows, n_rows)
    while n_rows % rows:
        rows //= 2
    return rows


class _Staged:
    def __init__(self, array, prefix=()):
        self.array, self.prefix = array, tuple(prefix)
        self.rows, self.cols = array.shape[len(self.prefix):]
        self.chunk = _stage_chunk_rows(self.rows, self.cols)

    def scratch(self):
        return [pltpu.VMEM((self.rows, self.cols), BF16),
                pltpu.VMEM((2, self.chunk, self.cols), self.array.dtype),
                pltpu.SemaphoreType.DMA((2,))]

    def load(self, hbm_ref, vmem_ref, stage_ref, sem_ref):
        src = hbm_ref.at[self.prefix] if self.prefix else hbm_ref
        n_chunks = self.rows // self.chunk

        def copy(c):
            return pltpu.make_async_copy(src.at[pl.ds(c * self.chunk, self.chunk)],
                                         stage_ref.at[c % 2], sem_ref.at[c % 2])

        copy(0).start()
        for c in range(n_chunks):
            if c + 1 < n_chunks:
                copy(c + 1).start()
            copy(c).wait()
            vmem_ref[c * self.chunk:(c + 1) * self.chunk, :] = stage_ref[c % 2].astype(BF16)


def _load_staged(staged, hbm_refs, scratch, first_step):
    groups = [scratch[3 * k:3 * k + 3] for k in range(len(staged))]

    @pl.when(first_step)
    def _():
        for w, hbm_ref, (vmem_ref, stage_ref, sem_ref) in zip(staged, hbm_refs, groups):
            w.load(hbm_ref, vmem_ref, stage_ref, sem_ref)

    return [g[0] for g in groups]


def _staged_specs(staged):
    in_specs = [pl.BlockSpec(memory_space=pl.ANY) for _ in staged]
    scratch = [s for w in staged for s in w.scratch()]
    return in_specs, [w.array for w in staged], scratch


def _gain_spec_d(row, d):
    return pl.BlockSpec((None, 1, d), lambda *_: (row, 0, 0))


def _params(n_grid):
    return pltpu.CompilerParams(
        dimension_semantics=("arbitrary",) * n_grid,
        vmem_limit_bytes=VMEM_LIMIT_BYTES,
    )


def _ffn_kernel(x_ref, gpre_ref, gpost_ref, wgu_hbm, wd_hbm, o_ref, *scratch,
                d_ff, n_chunks, n_sub, staged):
    wgu_ref, wd_ref = _load_staged(staged, (wgu_hbm, wd_hbm), scratch, pl.program_id(0) == 0)
    tm = x_ref.shape[0]
    ts = tm // n_sub
    fc = d_ff // n_chunks
    for sub in range(n_sub):
        rows = slice(sub * ts, (sub + 1) * ts)
        x = x_ref[rows, :]
        xn = _rms(x, gpre_ref[...]).astype(BF16)
        y = None
        for c in range(n_chunks):
            a = _dot(xn, wgu_ref[:, c * fc:(c + 1) * fc])
            b = _dot(xn, wgu_ref[:, d_ff + c * fc:d_ff + (c + 1) * fc])
            act = (jax.nn.silu(a) * b).astype(BF16)
            part = _dot(act, wd_ref[c * fc:(c + 1) * fc, :])
            y = part if y is None else y + part
        o_ref[rows, :] = x + 0.5 * _rms(y, gpost_ref[...])


def _ffn(x2, gains, row_pre, row_post, w_gu, w_down, *, tm, n_chunks, n_sub):
    t, d = x2.shape
    staged = (w_gu, w_down)
    w_specs, w_args, w_scratch = _staged_specs(staged)
    tile = pl.BlockSpec((tm, d), lambda i: (i, 0))
    return pl.pallas_call(
        functools.partial(_ffn_kernel, d_ff=w_down.rows, n_chunks=n_chunks, n_sub=n_sub,
                          staged=staged),
        grid=(t // tm,),
        in_specs=[tile, _gain_spec_d(row_pre, d), _gain_spec_d(row_post, d)] + w_specs,
        out_specs=tile,
        out_shape=jax.ShapeDtypeStruct((t, d), F32),
        scratch_shapes=w_scratch,
        compiler_params=_params(1),
        name="ffn",
    )(x2, gains, gains, *w_args)


def _ple_kernel(x_ref, p_ref, g_ref, wg_hbm, wp_hbm, o_ref, *scratch, staged):
    wg_ref, wp_ref = _load_staged(staged, (wg_hbm, wp_hbm), scratch, pl.program_id(0) == 0)
    x = x_ref[...]
    xn = _rms(x, g_ref[...]).astype(BF16)
    gate = jax.nn.sigmoid(_dot(xn, wg_ref[...]))
    proj = _dot(p_ref[...].astype(BF16), wp_ref[...])
    o_ref[...] = x + gate * proj


def _ple(x2, p3, layer, gains, row, w_gate, w_proj, *, tm):
    t, d = x2.shape
    pd = p3.shape[-1]
    staged = (w_gate, w_proj)
    w_specs, w_args, w_scratch = _staged_specs(staged)
    tile = pl.BlockSpec((tm, d), lambda i: (i, 0))
    return pl.pallas_call(
        functools.partial(_ple_kernel, staged=staged),
        grid=(t // tm,),
        in_specs=[tile, pl.BlockSpec((None, tm, pd), lambda i: (layer, i, 0)),
                  _gain_spec_d(row, d)] + w_specs,
        out_specs=tile,
        out_shape=jax.ShapeDtypeStruct((t, d), F32),
        scratch_shapes=w_scratch,
        compiler_params=_params(1),
        name="ple",
    )(x2, p3, gains, *w_args)


def _first_step_2d():
    return (pl.program_id(0) == 0) & (pl.program_id(1) == 0)


def _conv_kernel(x_ref, gpre_ref, gpost_ref, cw_ref, win_hbm, wout_hbm, o_ref, u_buf, *scratch,
                 n_sub, staged):
    win_ref, wout_ref = _load_staged(staged, (win_hbm, wout_hbm), scratch, _first_step_2d())
    tm, d = x_ref.shape
    ts = tm // n_sub

    @pl.when(pl.program_id(1) == 0)
    def _():
        u_buf[0:SUBLANES, :] = jnp.zeros((SUBLANES, d), F32)

    cw = cw_ref[...]
    u = None
    for sub in range(n_sub):
        r0 = sub * ts
        x = x_ref[r0:r0 + ts, :]
        h = _rms(x, gpre_ref[...]).astype(BF16)
        proj = _dot(h, win_ref[...])
        bgate, cgate, v = proj[:, :d], proj[:, d:2 * d], proj[:, 2 * d:]
        u = cgate * v
        u_buf[SUBLANES + r0:SUBLANES + r0 + ts, :] = u
        conv = (cw[2:3, :] * u
                + cw[1:2, :] * u_buf[SUBLANES - 1 + r0:SUBLANES - 1 + r0 + ts, :]
                + cw[0:1, :] * u_buf[SUBLANES - 2 + r0:SUBLANES - 2 + r0 + ts, :])
        y = _dot((bgate * conv).astype(BF16), wout_ref[...])
        o_ref[r0:r0 + ts, :] = x + _rms(y, gpost_ref[...])
    u_buf[0:SUBLANES, :] = u[ts - SUBLANES:, :]


def _conv_mixer(x3, gains, row_pre, row_post, w_in, conv_w, w_out, *, tm, n_sub):
    b, s, d = x3.shape
    staged = (w_in, w_out)
    w_specs, w_args, w_scratch = _staged_specs(staged)
    tile = pl.BlockSpec((None, tm, d), lambda i, j: (i, j, 0))
    return pl.pallas_call(
        functools.partial(_conv_kernel, n_sub=n_sub, staged=staged),
        grid=(b, s // tm),
        in_specs=[tile, _gain_spec_d(row_pre, d), _gain_spec_d(row_post, d),
                  _resident(conv_w.shape)] + w_specs,
        out_specs=tile,
        out_shape=jax.ShapeDtypeStruct((b, s, d), F32),
        scratch_shapes=[pltpu.VMEM((tm + SUBLANES, d), F32)] + w_scratch,
        compiler_params=_params(2),
        name="conv_mixer",
    )(x3, gains, gains, conv_w, *w_args)


def _rel_bucket(dist):
    max_exact = REL_BUCKETS // 2
    dd = jnp.maximum(dist, 1).astype(F32)
    large = max_exact + (jnp.log(dd / max_exact) / math.log(REL_MAX_DIST / max_exact)
                         * (REL_BUCKETS - max_exact)).astype(jnp.int32)
    large = jnp.minimum(large, REL_BUCKETS - 1)
    return jnp.where(dist < max_exact, dist, large)


def _bias_kernel(rb_ref, bucket_ref, o_ref):
    bucket = bucket_ref[...]
    qi = lax.broadcasted_iota(jnp.int32, bucket.shape, 0)
    kj = lax.broadcasted_iota(jnp.int32, bucket.shape, 1)
    causal = kj <= qi
    hits = [bucket == bkt for bkt in range(REL_BUCKETS)]
    neg = jnp.full(bucket.shape, -jnp.inf, F32)
    for h in range(ATT_HEADS):
        acc = jnp.zeros(bucket.shape, F32)
        for bkt in range(REL_BUCKETS):
            acc = jnp.where(hits[bkt], rb_ref[bkt, h], acc)
        o_ref[0, h] = jnp.where(causal, acc, neg)
        o_ref[1, h] = acc


def _bias_tables(rel_bias):
    assert WINDOW == BLOCK
    qi = jnp.arange(BLOCK)[:, None]
    kj = jnp.arange(BLOCK)[None, :]
    dist = jnp.where(kj <= qi, qi - kj, qi + BLOCK - kj)
    bucket = _rel_bucket(dist).astype(jnp.int32)
    out = pl.pallas_call(
        _bias_kernel,
        in_specs=[pl.BlockSpec(memory_space=pltpu.SMEM),
                  pl.BlockSpec(memory_space=pltpu.VMEM)],
        out_specs=pl.BlockSpec(memory_space=pltpu.VMEM),
        out_shape=jax.ShapeDtypeStruct((2, ATT_HEADS, BLOCK, BLOCK), F32),
        name="rel_bias_table",
    )(rel_bias, bucket)
    return out.reshape(2, ATT_KV_HEADS, ATT_GROUP * BLOCK, BLOCK)


def _swa_kernel(x_ref, xprev_ref, gpre_ref, gpost_ref, bias_ref, sink_ref, wqkv_hbm, wo_hbm, o_ref,
                kk_buf, vv_buf, att_buf, *scratch, staged):
    wqkv_ref, wo_ref = _load_staged(staged, (wqkv_hbm, wo_hbm), scratch, _first_step_2d())
    tm, d = x_ref.shape
    n_blk = tm // BLOCK
    dq = ATT_HEADS * ATT_HEAD_DIM
    dkv = ATT_KV_HEADS * ATT_HEAD_DIM
    first_tile = pl.program_id(1) == 0

    x = x_ref[...]
    h = _rms(x, gpre_ref[...]).astype(BF16)
    qkv = _dot(h, wqkv_ref[...])
    q = (qkv[:, :dq] * (ATT_HEAD_DIM ** -0.5)).astype(BF16)
    h_prev = _rms(xprev_ref[...], gpre_ref[...]).astype(BF16)
    kv_prev = _dot(h_prev, wqkv_ref[:, dq:])
    k = jnp.concatenate([kv_prev[:, :dkv], qkv[:, dq:dq + dkv]], axis=0)
    v = jnp.concatenate([kv_prev[:, dkv:], qkv[:, dq + dkv:]], axis=0)

    low = lax.broadcasted_iota(jnp.int32, k.shape, 1) < ATT_HEAD_DIM
    k_sw = pltpu.roll(k, ATT_HEAD_DIM, 1)
    v_sw = pltpu.roll(v, ATT_HEAD_DIM, 1)
    kk_buf[0] = jnp.where(low, k, k_sw).astype(BF16)
    kk_buf[1] = jnp.where(low, k_sw, k).astype(BF16)
    vv_buf[0, :, 0:LANES] = jnp.where(low, v, v_sw).astype(BF16)
    vv_buf[1, :, 0:LANES] = jnp.where(low, v_sw, v).astype(BF16)
    vv_buf[:, :, LANES:] = jnp.ones((ATT_KV_HEADS, tm + BLOCK, LANES), BF16)

    low_b = lax.broadcasted_iota(jnp.int32, (BLOCK, LANES), 1) < ATT_HEAD_DIM
    zero = jnp.zeros((BLOCK, LANES), BF16)
    pairs = ATT_GROUP // 2
    stacked = (2 * BLOCK, BLOCK)
    q_pos = lax.broadcasted_iota(jnp.int32, stacked, 0) & (BLOCK - 1)
    causal = lax.broadcasted_iota(jnp.int32, stacked, 1) <= q_pos
    zero_p = jnp.zeros(stacked, BF16)
    for n in range(n_blk):
        rows = slice(n * BLOCK, (n + 1) * BLOCK)
        variant = jnp.where(first_tile, 0, 1) if n == 0 else 1
        for hk in range(ATT_KV_HEADS):
            kb = kk_buf[hk, n * BLOCK:(n + 2) * BLOCK, :]
            vb = vv_buf[hk, n * BLOCK:(n + 2) * BLOCK, :]
            for j in range(pairs):
                col = (hk * pairs + j) * LANES
                pair_rows = slice(2 * j * BLOCK, (2 * j + 2) * BLOCK)
                q2 = q[rows, col:col + LANES]
                qs = jnp.concatenate([jnp.where(low_b, q2, zero), jnp.where(low_b, zero, q2)], axis=0)
                s2 = _dot_nt(qs, kb)
                logits = (jnp.where(causal, s2[:, BLOCK:], s2[:, :BLOCK])
                          + bias_ref[variant, hk, pair_rows, :])
                sink = sink_ref[hk, pair_rows, :]
                m = jnp.maximum(jnp.max(logits, axis=-1, keepdims=True), sink)
                e = jnp.exp(logits - m).astype(BF16)
                e2 = jnp.concatenate([jnp.where(causal, zero_p, e),
                                      jnp.where(causal, e, zero_p)], axis=1)
                r = _dot(e2, vb)
                out = r[:, :LANES] / (r[:, LANES:] + jnp.exp(sink - m))
                att_buf[rows, col:col + LANES] = jnp.where(low_b, out[:BLOCK], out[BLOCK:]).astype(BF16)

    y = _dot(att_buf[...], wo_ref[...])
    o_ref[...] = x + _rms(y, gpost_ref[...])


def _swa_mixer(x3, gains, row_pre, row_post, w_qkv, bias_tab, sinks, w_o, *, tm):
    b, s, d = x3.shape
    blocks_per_tile = tm // BLOCK
    tile = pl.BlockSpec((None, tm, d), lambda i, j: (i, j, 0))
    prev_block = pl.BlockSpec((None, BLOCK, d),
                              lambda i, j: (i, jnp.maximum(j * blocks_per_tile - 1, 0), 0))
    sink_col = jnp.repeat(sinks.astype(F32).reshape(ATT_KV_HEADS, ATT_GROUP), BLOCK, axis=1)
    sink_col = jnp.broadcast_to(sink_col[..., None], sink_col.shape + (LANES,))
    staged = (w_qkv, w_o)
    w_specs, w_args, w_scratch = _staged_specs(staged)
    return pl.pallas_call(
        functools.partial(_swa_kernel, staged=staged),
        grid=(b, s // tm),
        in_specs=[tile, prev_block, _gain_spec_d(row_pre, d), _gain_spec_d(row_post, d),
                  _resident(bias_tab.shape), _resident(sink_col.shape)] + w_specs,
        out_specs=tile,
        out_shape=jax.ShapeDtypeStruct((b, s, d), F32),
        scratch_shapes=[pltpu.VMEM((ATT_KV_HEADS, tm + BLOCK, LANES), BF16),
                        pltpu.VMEM((ATT_KV_HEADS, tm + BLOCK, 2 * LANES), BF16),
                        pltpu.VMEM((tm, ATT_HEADS * ATT_HEAD_DIM), BF16)] + w_scratch,
        compiler_params=_params(2),
        name="swa_mixer",
    )(x3, x3, gains, gains, bias_tab, sink_col, *w_args)


def _ret_tables(s, dk):
    f32 = F32
    half = dk // 2
    inv = ROPE_BASE ** (-jnp.arange(half, dtype=f32) / half)
    ang = jnp.arange(s)[:, None].astype(f32) * inv[None, :]
    c = RET_CHUNK
    log_g = jnp.log(1.0 - 2.0 ** (-5.0 - jnp.arange(RET_HEADS, dtype=f32)))
    idx = jnp.arange(c, dtype=f32)
    diff = idx[:, None] - idx[None, :]
    decay_mask = jnp.where(diff >= 0, jnp.exp(log_g[:, None, None] * jnp.maximum(diff, 0.0)), 0.0)
    q_decay = jnp.exp(log_g[:, None] * (idx + 1.0))[..., None]
    k_decay = jnp.exp(log_g[:, None] * (c - 1.0 - idx))[..., None]
    chunk_decay = jnp.exp(log_g * c)
    return jnp.cos(ang), jnp.sin(ang), decay_mask, q_decay, k_decay, chunk_decay


def _ret_kernel(x_ref, gpre_ref, gpost_ref, cos_ref, sin_ref, dmask_ref, qdec_ref, kdec_ref,
                cdec_ref, w_hbm, wo_hbm, o_ref, state_ref, gated_buf, *scratch,
                dk, dv, n_sub, staged):
    w_ref, wo_ref = _load_staged(staged, (w_hbm, wo_hbm), scratch, _first_step_2d())
    tm, d = x_ref.shape
    ts = tm // n_sub
    n_chunk = ts // RET_CHUNK
    half = dk // 2
    hq = RET_HEADS * dk

    @pl.when(pl.program_id(1) == 0)
    def _():
        state_ref[...] = jnp.zeros(state_ref.shape, F32)

    for sub in range(n_sub):
        r0 = sub * ts
        x = x_ref[r0:r0 + ts, :]
        h = _rms(x, gpre_ref[...]).astype(BF16)
        proj = _dot(h, w_ref[...])
        cos = cos_ref[r0:r0 + ts, :]
        sin = sin_ref[r0:r0 + ts, :]

        def rot(t):
            t1, t2 = t[:, :half], t[:, half:]
            return jnp.concatenate([t1 * cos - t2 * sin, t1 * sin + t2 * cos], axis=-1)

        for hd in range(RET_HEADS):
            q = rot(proj[:, hd * dk:(hd + 1) * dk])
            k = rot(proj[:, hq + hd * dk:hq + (hd + 1) * dk]) * (dk ** -0.5)
            v = proj[:, 2 * hq + hd * dv:2 * hq + (hd + 1) * dv]
            g = proj[:, 2 * hq + RET_HEADS * dv + hd * dv:2 * hq + RET_HEADS * dv + (hd + 1) * dv]
            dmask = dmask_ref[hd]
            qdec = qdec_ref[hd]
            kdec = kdec_ref[hd]
            cdec = cdec_ref[hd]
            state = state_ref[hd]
            for c in range(n_chunk):
                rows = slice(c * RET_CHUNK, (c + 1) * RET_CHUNK)
                qc = q[rows].astype(BF16)
                kc = k[rows]
                vc = v[rows].astype(BF16)
                inner = _dot_nt(qc, kc.astype(BF16)) * dmask
                o = _dot(inner.astype(BF16), vc) + _dot(qc, state.astype(BF16)) * qdec
                state = state * cdec + _dot_tn((kc * kdec).astype(BF16), vc)
                mu = jnp.mean(o, axis=-1, keepdims=True)
                oc = o - mu
                var = jnp.mean(oc * oc, axis=-1, keepdims=True)
                on = oc * lax.rsqrt(var + NORM_EPS)
                gated_buf[r0 + c * RET_CHUNK:r0 + (c + 1) * RET_CHUNK, hd * dv:(hd + 1) * dv] = (
                    jax.nn.silu(g[rows]) * on).astype(BF16)
            state_ref[hd] = state

        y = _dot(gated_buf[r0:r0 + ts, :], wo_ref[...])
        o_ref[r0:r0 + ts, :] = x + _rms(y, gpost_ref[...])


def _ret_mixer(x3, gains, row_pre, row_post, w_qkvg, w_o, *, tm, n_sub):
    b, s, d = x3.shape
    dk = d // RET_HEADS
    dv = 2 * dk
    cos, sin, dmask, qdec, kdec, cdec = _ret_tables(s, dk)
    cdec = cdec.reshape(RET_HEADS, 1, 1)
    tile = pl.BlockSpec((None, tm, d), lambda i, j: (i, j, 0))
    rope = pl.BlockSpec((tm, dk // 2), lambda i, j: (j, 0))
    staged = (w_qkvg, w_o)
    w_specs, w_args, w_scratch = _staged_specs(staged)
    return pl.pallas_call(
        functools.partial(_ret_kernel, dk=dk, dv=dv, n_sub=n_sub, staged=staged),
        grid=(b, s // tm),
        in_specs=[tile, _gain_spec_d(row_pre, d), _gain_spec_d(row_post, d),
                  rope, rope, _resident(dmask.shape),
                  _resident(qdec.shape), _resident(kdec.shape), _resident(cdec.shape)] + w_specs,
        out_specs=tile,
        out_shape=jax.ShapeDtypeStruct((b, s, d), F32),
        scratch_shapes=[pltpu.VMEM((RET_HEADS, dk, dv), F32),
                        pltpu.VMEM((tm, RET_HEADS * dv), BF16)] + w_scratch,
        compiler_params=_params(2),
        name="ret_mixer",
    )(x3, gains, gains, cos, sin, dmask, qdec, kdec, cdec, *w_args)


def _tile(n, want):
    t = min(n, want)
    assert n % t == 0, (n, t)
    return t


def kernel(x, p, norm_g, ffn_w_gu, ffn_w_down, ple_w_proj, ple_w_gate, rel_bias, conv_w_in, conv_w,
           conv_w_out, swa_w_qkv, swa_sinks, swa_w_o, ret_w_qkvg, ret_w_o):
    depth = norm_g.shape[0]
    b, s, d = x.shape
    t = b * s
    gains = norm_g.astype(F32).reshape(depth * N_NORMS, 1, d)
    p3 = p.reshape(depth, t, p.shape[-1])
    bias_tab = _bias_tables(rel_bias.astype(F32)) if depth > 1 else None

    tm_ffn = _tile(t, 1024)
    tm_ple = _tile(t, 1024)
    tm_conv = _tile(s, 512)
    tm_swa = _tile(s, 512)
    tm_ret = _tile(s, 512)

    for i in range(depth):
        kind, j = i % N_MIXERS, i // N_MIXERS
        r = i * N_NORMS
        x2 = _ffn(x.reshape(t, d), gains, r + 0, r + 1,
                  _Staged(ffn_w_gu, (i, 0)), _Staged(ffn_w_down, (i, 0)),
                  tm=tm_ffn, n_chunks=1, n_sub=4)
        x = x2.reshape(b, s, d)
        if kind == 0:
            x = _conv_mixer(x, gains, r + 2, r + 3, _Staged(conv_w_in, (j,)), conv_w[j].astype(F32),
                            _Staged(conv_w_out, (j,)), tm=tm_conv, n_sub=2)
        elif kind == 1:
            x = _swa_mixer(x, gains, r + 2, r + 3, _Staged(swa_w_qkv, (j,)), bias_tab, swa_sinks[j],
                           _Staged(swa_w_o, (j,)), tm=tm_swa)
        else:
            x = _ret_mixer(x, gains, r + 2, r + 3, _Staged(ret_w_qkvg, (j,)), _Staged(ret_w_o, (j,)),
                           tm=tm_ret, n_sub=2)
        x2 = _ffn(x.reshape(t, d), gains, r + 4, r + 5,
                  _Staged(ffn_w_gu, (i, 1)), _Staged(ffn_w_down, (i, 1)),
                  tm=tm_ffn, n_chunks=1, n_sub=4)
        x2 = _ple(x2, p3, i, gains, r + 6, _Staged(ple_w_gate, (i,)), _Staged(ple_w_proj, (i,)),
                  tm=tm_ple)
        x = x2.reshape(b, s, d)
    return x
```

```python
import functools
import math

import jax
import jax.numpy as jnp
from jax import lax
from jax.experimental import pallas as pl
from jax.experimental.pallas import tpu as pltpu

N_MIXERS = 3
N_NORMS = 7
NORM_EPS = 1e-6
CONV_WIDTH = 3
ATT_HEADS = 16
ATT_KV_HEADS = 2
ATT_HEAD_DIM = 64
ATT_GROUP = ATT_HEADS // ATT_KV_HEADS
WINDOW = 128
BLOCK = 128
REL_BUCKETS = 32
REL_MAX_DIST = 128
RET_HEADS = 4
RET_CHUNK = 256
ROPE_BASE = 10000.0

LANES = 128
SUBLANES = 8
VMEM_LIMIT_BYTES = 56 * 1024 * 1024
STAGE_SLOT_ELEMS = 384 * 1024
STAGE_SLOTS = 4

F32 = jnp.float32
BF16 = jnp.bfloat16


def _rms(x, g):
    ms = jnp.mean(x * x, axis=-1, keepdims=True)
    return x * lax.rsqrt(ms + NORM_EPS) * g


def _dot(a, b):
    return jnp.dot(a, b, preferred_element_type=F32)


def _dot_nt(a, b):
    return lax.dot_general(a, b, (((1,), (1,)), ((), ())), preferred_element_type=F32)


def _dot_tn(a, b):
    return lax.dot_general(a, b, (((0,), (0,)), ((), ())), preferred_element_type=F32)


def _resident(shape):
    nd = len(shape)
    return pl.BlockSpec(shape, lambda *_: (0,) * nd, pipeline_mode=pl.Buffered(1))


def _stage_chunk_rows(n_rows, n_cols):
    rows = 1 << ((STAGE_SLOT_ELEMS // n_cols).bit_length() - 1)
    rows = min(rows, n_rows)
    while n_rows % rows:
        rows //= 2
    return rows


class _Staged:
    def __init__(self, array, prefix=()):
        self.array, self.prefix = array, tuple(prefix)
        self.rows, self.cols = array.shape[len(self.prefix):]
        self.chunk = _stage_chunk_rows(self.rows, self.cols)

    def scratch(self):
        return [pltpu.VMEM((self.rows, self.cols), BF16),
                pltpu.VMEM((STAGE_SLOTS, self.chunk, self.cols), self.array.dtype),
                pltpu.SemaphoreType.DMA((STAGE_SLOTS,))]

    def load(self, hbm_ref, vmem_ref, stage_ref, sem_ref):
        src = hbm_ref.at[self.prefix] if self.prefix else hbm_ref
        n_chunks = self.rows // self.chunk
        ahead = STAGE_SLOTS - 1

        def copy(c):
            slot = c % STAGE_SLOTS
            return pltpu.make_async_copy(src.at[pl.ds(c * self.chunk, self.chunk)],
                                         stage_ref.at[slot], sem_ref.at[slot])

        for c in range(min(ahead, n_chunks)):
            copy(c).start()
        for c in range(n_chunks):
            if c + ahead < n_chunks:
                copy(c + ahead).start()
            copy(c).wait()
            vmem_ref[c * self.chunk:(c + 1) * self.chunk, :] = (
                stage_ref[c % STAGE_SLOTS].astype(BF16))


def _load_staged(staged, hbm_refs, scratch, first_step):
    groups = [scratch[3 * k:3 * k + 3] for k in range(len(staged))]

    @pl.when(first_step)
    def _():
        for w, hbm_ref, (vmem_ref, stage_ref, sem_ref) in zip(staged, hbm_refs, groups):
            w.load(hbm_ref, vmem_ref, stage_ref, sem_ref)

    return [g[0] for g in groups]


def _staged_specs(staged):
    in_specs = [pl.BlockSpec(memory_space=pl.ANY) for _ in staged]
    scratch = [s for w in staged for s in w.scratch()]
    return in_specs, [w.array for w in staged], scratch


def _gain_spec_d(row, d):
    return pl.BlockSpec((None, 1, d), lambda *_: (row, 0, 0))


def _params(n_grid):
    return pltpu.CompilerParams(
        dimension_semantics=("arbitrary",) * n_grid,
        vmem_limit_bytes=VMEM_LIMIT_BYTES,
    )


def _ffn_kernel(x_ref, gpre_ref, gpost_ref, wgu_hbm, wd_hbm, o_ref, *scratch,
                d_ff, n_chunks, n_sub, staged):
    wgu_ref, wd_ref = _load_staged(staged, (wgu_hbm, wd_hbm), scratch, pl.program_id(0) == 0)
    tm = x_ref.shape[0]
    ts = tm // n_sub
    fc = d_ff // n_chunks
    for sub in range(n_sub):
        rows = slice(sub * ts, (sub + 1) * ts)
        x = x_ref[rows, :]
        xn = _rms(x, gpre_ref[...]).astype(BF16)
        y = None
        for c in range(n_chunks):
            a = _dot(xn, wgu_ref[:, c * fc:(c + 1) * fc])
            b = _dot(xn, wgu_ref[:, d_ff + c * fc:d_ff + (c + 1) * fc])
            act = (jax.nn.silu(a) * b).astype(BF16)
            part = _dot(act, wd_ref[c * fc:(c + 1) * fc, :])
            y = part if y is None else y + part
        o_ref[rows, :] = x + 0.5 * _rms(y, gpost_ref[...])


def _ffn(x2, gains, row_pre, row_post, w_gu, w_down, *, tm, n_chunks, n_sub):
    t, d = x2.shape
    staged = (w_gu, w_down)
    w_specs, w_args, w_scratch = _staged_specs(staged)
    tile = pl.BlockSpec((tm, d), lambda i: (i, 0))
    return pl.pallas_call(
        functools.partial(_ffn_kernel, d_ff=w_down.rows, n_chunks=n_chunks, n_sub=n_sub,
                          staged=staged),
        grid=(t // tm,),
        in_specs=[tile, _gain_spec_d(row_pre, d), _gain_spec_d(row_post, d)] + w_specs,
        out_specs=tile,
        out_shape=jax.ShapeDtypeStruct((t, d), F32),
        scratch_shapes=w_scratch,
        compiler_params=_params(1),
        name="ffn",
    )(x2, gains, gains, *w_args)


def _ple_kernel(x_ref, p_ref, g_ref, wg_hbm, wp_hbm, o_ref, *scratch, n_sub, staged):
    wg_ref, wp_ref = _load_staged(staged, (wg_hbm, wp_hbm), scratch, pl.program_id(0) == 0)
    ts = x_ref.shape[0] // n_sub
    for sub in range(n_sub):
        rows = slice(sub * ts, (sub + 1) * ts)
        x = x_ref[rows, :]
        xn = _rms(x, g_ref[...]).astype(BF16)
        gate = jax.nn.sigmoid(_dot(xn, wg_ref[...]))
        proj = _dot(p_ref[rows, :].astype(BF16), wp_ref[...])
        o_ref[rows, :] = x + gate * proj


def _ple(x2, p3, layer, gains, row, w_gate, w_proj, *, tm, n_sub):
    t, d = x2.shape
    pd = p3.shape[-1]
    staged = (w_gate, w_proj)
    w_specs, w_args, w_scratch = _staged_specs(staged)
    tile = pl.BlockSpec((tm, d), lambda i: (i, 0))
    return pl.pallas_call(
        functools.partial(_ple_kernel, n_sub=n_sub, staged=staged),
        grid=(t // tm,),
        in_specs=[tile, pl.BlockSpec((None, tm, pd), lambda i: (layer, i, 0)),
                  _gain_spec_d(row, d)] + w_specs,
        out_specs=tile,
        out_shape=jax.ShapeDtypeStruct((t, d), F32),
        scratch_shapes=w_scratch,
        compiler_params=_params(1),
        name="ple",
    )(x2, p3, gains, *w_args)


def _first_step_2d():
    return (pl.program_id(0) == 0) & (pl.program_id(1) == 0)


def _conv_kernel(x_ref, gpre_ref, gpost_ref, cw_ref, win_hbm, wout_hbm, o_ref, u_buf, *scratch,
                 n_sub, staged):
    win_ref, wout_ref = _load_staged(staged, (win_hbm, wout_hbm), scratch, _first_step_2d())
    tm, d = x_ref.shape
    ts = tm // n_sub

    @pl.when(pl.program_id(1) == 0)
    def _():
        u_buf[0:SUBLANES, :] = jnp.zeros((SUBLANES, d), F32)

    cw = cw_ref[...]
    u = None
    for sub in range(n_sub):
        r0 = sub * ts
        x = x_ref[r0:r0 + ts, :]
        h = _rms(x, gpre_ref[...]).astype(BF16)
        proj = _dot(h, win_ref[...])
        bgate, cgate, v = proj[:, :d], proj[:, d:2 * d], proj[:, 2 * d:]
        u = cgate * v
        u_buf[SUBLANES + r0:SUBLANES + r0 + ts, :] = u
        conv = (cw[2:3, :] * u
                + cw[1:2, :] * u_buf[SUBLANES - 1 + r0:SUBLANES - 1 + r0 + ts, :]
                + cw[0:1, :] * u_buf[SUBLANES - 2 + r0:SUBLANES - 2 + r0 + ts, :])
        y = _dot((bgate * conv).astype(BF16), wout_ref[...])
        o_ref[r0:r0 + ts, :] = x + _rms(y, gpost_ref[...])
    u_buf[0:SUBLANES, :] = u[ts - SUBLANES:, :]


def _conv_mixer(x3, gains, row_pre, row_post, w_in, conv_w, w_out, *, tm, n_sub):
    b, s, d = x3.shape
    staged = (w_in, w_out)
    w_specs, w_args, w_scratch = _staged_specs(staged)
    tile = pl.BlockSpec((None, tm, d), lambda i, j: (i, j, 0))
    return pl.pallas_call(
        functools.partial(_conv_kernel, n_sub=n_sub, staged=staged),
        grid=(b, s // tm),
        in_specs=[tile, _gain_spec_d(row_pre, d), _gain_spec_d(row_post, d),
                  _resident(conv_w.shape)] + w_specs,
        out_specs=tile,
        out_shape=jax.ShapeDtypeStruct((b, s, d), F32),
        scratch_shapes=[pltpu.VMEM((tm + SUBLANES, d), F32)] + w_scratch,
        compiler_params=_params(2),
        name="conv_mixer",
    )(x3, gains, gains, conv_w, *w_args)


def _rel_bucket(dist):
    max_exact = REL_BUCKETS // 2
    dd = jnp.maximum(dist, 1).astype(F32)
    large = max_exact + (jnp.log(dd / max_exact) / math.log(REL_MAX_DIST / max_exact)
                         * (REL_BUCKETS - max_exact)).astype(jnp.int32)
    large = jnp.minimum(large, REL_BUCKETS - 1)
    return jnp.where(dist < max_exact, dist, large)


def _bias_kernel(rb_ref, bucket_ref, o_ref):
    bucket = bucket_ref[...]
    qi = lax.broadcasted_iota(jnp.int32, bucket.shape, 0)
    kj = lax.broadcasted_iota(jnp.int32, bucket.shape, 1)
    causal = kj <= qi
    hits = [bucket == bkt for bkt in range(REL_BUCKETS)]
    neg = jnp.full(bucket.shape, -jnp.inf, F32)
    for h in range(ATT_HEADS):
        acc = jnp.zeros(bucket.shape, F32)
        for bkt in range(REL_BUCKETS):
            acc = jnp.where(hits[bkt], rb_ref[bkt, h], acc)
        o_ref[0, h] = jnp.where(causal, acc, neg)
        o_ref[1, h] = acc


def _bias_tables(rel_bias):
    assert WINDOW == BLOCK
    qi = jnp.arange(BLOCK)[:, None]
    kj = jnp.arange(BLOCK)[None, :]
    dist = jnp.where(kj <= qi, qi - kj, qi + BLOCK - kj)
    bucket = _rel_bucket(dist).astype(jnp.int32)
    out = pl.pallas_call(
        _bias_kernel,
        in_specs=[pl.BlockSpec(memory_space=pltpu.SMEM),
                  pl.BlockSpec(memory_space=pltpu.VMEM)],
        out_specs=pl.BlockSpec(memory_space=pltpu.VMEM),
        out_shape=jax.ShapeDtypeStruct((2, ATT_HEADS, BLOCK, BLOCK), F32),
        name="rel_bias_table",
    )(rel_bias, bucket)
    return out.reshape(2, ATT_KV_HEADS, ATT_GROUP * BLOCK, BLOCK)


def _swa_kernel(x_ref, xprev_ref, gpre_ref, gpost_ref, bias_ref, sink_ref, wqkv_hbm, wo_hbm, o_ref,
                kk_buf, vv_buf, att_buf, *scratch, staged):
    wqkv_ref, wo_ref = _load_staged(staged, (wqkv_hbm, wo_hbm), scratch, _first_step_2d())
    tm, d = x_ref.shape
    n_blk = tm // BLOCK
    dq = ATT_HEADS * ATT_HEAD_DIM
    dkv = ATT_KV_HEADS * ATT_HEAD_DIM
    first_tile = pl.program_id(1) == 0

    x = x_ref[...]
    h = _rms(x, gpre_ref[...]).astype(BF16)
    qkv = _dot(h, wqkv_ref[...])
    q = (qkv[:, :dq] * (ATT_HEAD_DIM ** -0.5)).astype(BF16)
    h_prev = _rms(xprev_ref[...], gpre_ref[...]).astype(BF16)
    kv_prev = _dot(h_prev, wqkv_ref[:, dq:])
    k = jnp.concatenate([kv_prev[:, :dkv], qkv[:, dq:dq + dkv]], axis=0)
    v = jnp.concatenate([kv_prev[:, dkv:], qkv[:, dq + dkv:]], axis=0)

    low = lax.broadcasted_iota(jnp.int32, k.shape, 1) < ATT_HEAD_DIM
    k_sw = pltpu.roll(k, ATT_HEAD_DIM, 1)
    v_sw = pltpu.roll(v, ATT_HEAD_DIM, 1)
    kk_buf[0] = jnp.where(low, k, k_sw).astype(BF16)
    kk_buf[1] = jnp.where(low, k_sw, k).astype(BF16)
    vv_buf[0, :, 0:LANES] = jnp.where(low, v, v_sw).astype(BF16)
    vv_buf[1, :, 0:LANES] = jnp.where(low, v_sw, v).astype(BF16)
    vv_buf[:, :, LANES:] = jnp.ones((ATT_KV_HEADS, tm + BLOCK, LANES), BF16)

    low_b = lax.broadcasted_iota(jnp.int32, (BLOCK, LANES), 1) < ATT_HEAD_DIM
    zero = jnp.zeros((BLOCK, LANES), BF16)
    pairs = ATT_GROUP // 2
    stacked = (2 * BLOCK, BLOCK)
    q_pos = lax.broadcasted_iota(jnp.int32, stacked, 0) & (BLOCK - 1)
    causal = lax.broadcasted_iota(jnp.int32, stacked, 1) <= q_pos
    zero_p = jnp.zeros(stacked, BF16)
    for n in range(n_blk):
        rows = slice(n * BLOCK, (n + 1) * BLOCK)
        variant = jnp.where(first_tile, 0, 1) if n == 0 else 1
        for hk in range(ATT_KV_HEADS):
            kb = kk_buf[hk, n * BLOCK:(n + 2) * BLOCK, :]
            vb = vv_buf[hk, n * BLOCK:(n + 2) * BLOCK, :]
            for j in range(pairs):
                col = (hk * pairs + j) * LANES
                pair_rows = slice(2 * j * BLOCK, (2 * j + 2) * BLOCK)
                q2 = q[rows, col:col + LANES]
                qs = jnp.concatenate([jnp.where(low_b, q2, zero), jnp.where(low_b, zero, q2)], axis=0)
                s2 = _dot_nt(qs, kb)
                logits = (jnp.where(causal, s2[:, BLOCK:], s2[:, :BLOCK])
                          + bias_ref[variant, hk, pair_rows, :])
                sink = sink_ref[hk, pair_rows, :]
                m = jnp.maximum(jnp.max(logits, axis=-1, keepdims=True), sink)
                e = jnp.exp(logits - m).astype(BF16)
                e2 = jnp.concatenate([jnp.where(causal, zero_p, e),
                                      jnp.where(causal, e, zero_p)], axis=1)
                r = _dot(e2, vb)
                out = r[:, :LANES] / (r[:, LANES:] + jnp.exp(sink - m))
                att_buf[rows, col:col + LANES] = jnp.where(low_b, out[:BLOCK], out[BLOCK:]).astype(BF16)

    y = _dot(att_buf[...], wo_ref[...])
    o_ref[...] = x + _rms(y, gpost_ref[...])


def _swa_mixer(x3, gains, row_pre, row_post, w_qkv, bias_tab, sinks, w_o, *, tm):
    b, s, d = x3.shape
    blocks_per_tile = tm // BLOCK
    tile = pl.BlockSpec((None, tm, d), lambda i, j: (i, j, 0))
    prev_block = pl.BlockSpec((None, BLOCK, d),
                              lambda i, j: (i, jnp.maximum(j * blocks_per_tile - 1, 0), 0))
    sink_col = jnp.repeat(sinks.astype(F32).reshape(ATT_KV_HEADS, ATT_GROUP), BLOCK, axis=1)
    sink_col = jnp.broadcast_to(sink_col[..., None], sink_col.shape + (LANES,))
    staged = (w_qkv, w_o)
    w_specs, w_args, w_scratch = _staged_specs(staged)
    return pl.pallas_call(
        functools.partial(_swa_kernel, staged=staged),
        grid=(b, s // tm),
        in_specs=[tile, prev_block, _gain_spec_d(row_pre, d), _gain_spec_d(row_post, d),
                  _resident(bias_tab.shape), _resident(sink_col.shape)] + w_specs,
        out_specs=tile,
        out_shape=jax.ShapeDtypeStruct((b, s, d), F32),
        scratch_shapes=[pltpu.VMEM((ATT_KV_HEADS, tm + BLOCK, LANES), BF16),
                        pltpu.VMEM((ATT_KV_HEADS, tm + BLOCK, 2 * LANES), BF16),
                        pltpu.VMEM((tm, ATT_HEADS * ATT_HEAD_DIM), BF16)] + w_scratch,
        compiler_params=_params(2),
        name="swa_mixer",
    )(x3, x3, gains, gains, bias_tab, sink_col, *w_args)


def _ret_tables(s, dk):
    f32 = F32
    half = dk // 2
    inv = ROPE_BASE ** (-jnp.arange(half, dtype=f32) / half)
    ang = jnp.arange(s)[:, None].astype(f32) * inv[None, :]
    c = RET_CHUNK
    log_g = jnp.log(1.0 - 2.0 ** (-5.0 - jnp.arange(RET_HEADS, dtype=f32)))
    idx = jnp.arange(c, dtype=f32)
    diff = idx[:, None] - idx[None, :]
    decay_mask = jnp.where(diff >= 0, jnp.exp(log_g[:, None, None] * jnp.maximum(diff, 0.0)), 0.0)
    q_decay = jnp.exp(log_g[:, None] * (idx + 1.0))[..., None]
    k_decay = jnp.exp(log_g[:, None] * (c - 1.0 - idx))[..., None]
    chunk_decay = jnp.exp(log_g * c)
    return jnp.cos(ang), jnp.sin(ang), decay_mask, q_decay, k_decay, chunk_decay


def _ret_kernel(x_ref, gpre_ref, gpost_ref, cos_ref, sin_ref, dmask_ref, qdec_ref, kdec_ref,
                cdec_ref, w_hbm, wo_hbm, o_ref, state_ref, gated_buf, *scratch,
                dk, dv, n_sub, staged):
    w_ref, wo_ref = _load_staged(staged, (w_hbm, wo_hbm), scratch, _first_step_2d())
    tm, d = x_ref.shape
    ts = tm // n_sub
    n_chunk = ts // RET_CHUNK
    half = dk // 2
    hq = RET_HEADS * dk

    @pl.when(pl.program_id(1) == 0)
    def _():
        state_ref[...] = jnp.zeros(state_ref.shape, F32)

    for sub in range(n_sub):
        r0 = sub * ts
        x = x_ref[r0:r0 + ts, :]
        h = _rms(x, gpre_ref[...]).astype(BF16)
        proj = _dot(h, w_ref[...])
        cos = cos_ref[r0:r0 + ts, :]
        sin = sin_ref[r0:r0 + ts, :]

        def rot(t):
            t1, t2 = t[:, :half], t[:, half:]
            return jnp.concatenate([t1 * cos - t2 * sin, t1 * sin + t2 * cos], axis=-1)

        for hd in range(RET_HEADS):
            q = rot(proj[:, hd * dk:(hd + 1) * dk])
            k = rot(proj[:, hq + hd * dk:hq + (hd + 1) * dk]) * (dk ** -0.5)
            v = proj[:, 2 * hq + hd * dv:2 * hq + (hd + 1) * dv]
            g = proj[:, 2 * hq + RET_HEADS * dv + hd * dv:2 * hq + RET_HEADS * dv + (hd + 1) * dv]
            dmask = dmask_ref[hd]
            qdec = qdec_ref[hd]
            kdec = kdec_ref[hd]
            cdec = cdec_ref[hd]
            state = state_ref[hd]
            for c in range(n_chunk):
                rows = slice(c * RET_CHUNK, (c + 1) * RET_CHUNK)
                qc = q[rows].astype(BF16)
                kc = k[rows]
                vc = v[rows].astype(BF16)
                inner = _dot_nt(qc, kc.astype(BF16)) * dmask
                o = _dot(inner.astype(BF16), vc) + _dot(qc, state.astype(BF16)) * qdec
                state = state * cdec + _dot_tn((kc * kdec).astype(BF16), vc)
                mu = jnp.mean(o, axis=-1, keepdims=True)
                oc = o - mu
                var = jnp.mean(oc * oc, axis=-1, keepdims=True)
                on = oc * lax.rsqrt(var + NORM_EPS)
                gated_buf[r0 + c * RET_CHUNK:r0 + (c + 1) * RET_CHUNK, hd * dv:(hd + 1) * dv] = (
                    jax.nn.silu(g[rows]) * on).astype(BF16)
            state_ref[hd] = state

        y = _dot(gated_buf[r0:r0 + ts, :], wo_ref[...])
        o_ref[r0:r0 + ts, :] = x + _rms(y, gpost_ref[...])


def _ret_mixer(x3, gains, row_pre, row_post, w_qkvg, w_o, *, tm, n_sub):
    b, s, d = x3.shape
    dk = d // RET_HEADS
    dv = 2 * dk
    cos, sin, dmask, qdec, kdec, cdec = _ret_tables(s, dk)
    cdec = cdec.reshape(RET_HEADS, 1, 1)
    tile = pl.BlockSpec((None, tm, d), lambda i, j: (i, j, 0))
    rope = pl.BlockSpec((tm, dk // 2), lambda i, j: (j, 0))
    staged = (w_qkvg, w_o)
    w_specs, w_args, w_scratch = _staged_specs(staged)
    return pl.pallas_call(
        functools.partial(_ret_kernel, dk=dk, dv=dv, n_sub=n_sub, staged=staged),
        grid=(b, s // tm),
        in_specs=[tile, _gain_spec_d(row_pre, d), _gain_spec_d(row_post, d),
                  rope, rope, _resident(dmask.shape),
                  _resident(qdec.shape), _resident(kdec.shape), _resident(cdec.shape)] + w_specs,
        out_specs=tile,
        out_shape=jax.ShapeDtypeStruct((b, s, d), F32),
        scratch_shapes=[pltpu.VMEM((RET_HEADS, dk, dv), F32),
                        pltpu.VMEM((tm, RET_HEADS * dv), BF16)] + w_scratch,
        compiler_params=_params(2),
        name="ret_mixer",
    )(x3, gains, gains, cos, sin, dmask, qdec, kdec, cdec, *w_args)


def _tile(n, want):
    t = min(n, want)
    assert n % t == 0, (n, t)
    return t


def kernel(x, p, norm_g, ffn_w_gu, ffn_w_down, ple_w_proj, ple_w_gate, rel_bias, conv_w_in, conv_w,
           conv_w_out, swa_w_qkv, swa_sinks, swa_w_o, ret_w_qkvg, ret_w_o):
    depth = norm_g.shape[0]
    b, s, d = x.shape
    t = b * s
    gains = norm_g.astype(F32).reshape(depth * N_NORMS, 1, d)
    p3 = p.reshape(depth, t, p.shape[-1])
    bias_tab = _bias_tables(rel_bias.astype(F32)) if depth > 1 else None

    tm_ffn = _tile(t, 1024)
    tm_ple = _tile(t, 1024)
    tm_conv = _tile(s, 512)
    tm_swa = _tile(s, 512)
    tm_ret = _tile(s, 512)

    for i in range(depth):
        kind, j = i % N_MIXERS, i // N_MIXERS
        r = i * N_NORMS
        x2 = _ffn(x.reshape(t, d), gains, r + 0, r + 1,
                  _Staged(ffn_w_gu, (i, 0)), _Staged(ffn_w_down, (i, 0)),
                  tm=tm_ffn, n_chunks=1, n_sub=4)
        x = x2.reshape(b, s, d)
        if kind == 0:
            x = _conv_mixer(x, gains, r + 2, r + 3, _Staged(conv_w_in, (j,)), conv_w[j].astype(F32),
                            _Staged(conv_w_out, (j,)), tm=tm_conv, n_sub=2)
        elif kind == 1:
            x = _swa_mixer(x, gains, r + 2, r + 3, _Staged(swa_w_qkv, (j,)), bias_tab, swa_sinks[j],
                           _Staged(swa_w_o, (j,)), tm=tm_swa)
        else:
            x = _ret_mixer(x, gains, r + 2, r + 3, _Staged(ret_w_qkvg, (j,)), _Staged(ret_w_o, (j,)),
                           tm=tm_ret, n_sub=2)
        x2 = _ffn(x.reshape(t, d), gains, r + 4, r + 5,
                  _Staged(ffn_w_gu, (i, 1)), _Staged(ffn_w_down, (i, 1)),
                  tm=tm_ffn, n_chunks=1, n_sub=4)
        x2 = _ple(x2, p3, i, gains, r + 6, _Staged(ple_w_gate, (i,)), _Staged(ple_w_proj, (i,)),
                  tm=tm_ple, n_sub=4)
        x = x2.reshape(b, s, d)
    return x
```

```python
import functools
import math

import jax
import jax.numpy as jnp
from jax import lax
from jax.experimental import pallas as pl
from jax.experimental.pallas import tpu as pltpu

N_MIXERS = 3
N_NORMS = 7
NORM_EPS = 1e-6
CONV_WIDTH = 3
ATT_HEADS = 16
ATT_KV_HEADS = 2
ATT_HEAD_DIM = 64
ATT_GROUP = ATT_HEADS // ATT_KV_HEADS
WINDOW = 128
BLOCK = 128
REL_BUCKETS = 32
REL_MAX_DIST = 128
RET_HEADS = 4
RET_CHUNK = 256
ROPE_BASE = 10000.0

LANES = 128
SUBLANES = 8
VMEM_LIMIT_BYTES = 56 * 1024 * 1024
STAGE_SLOT_ELEMS = 192 * 1024
STAGE_SLOTS = 4

F32 = jnp.float32
BF16 = jnp.bfloat16


def _rms(x, g):
    ms = jnp.mean(x * x, axis=-1, keepdims=True)
    return x * lax.rsqrt(ms + NORM_EPS) * g


def _dot(a, b):
    return jnp.dot(a, b, preferred_element_type=F32)


def _dot_nt(a, b):
    return lax.dot_general(a, b, (((1,), (1,)), ((), ())), preferred_element_type=F32)


def _dot_tn(a, b):
    return lax.dot_general(a, b, (((0,), (0,)), ((), ())), preferred_element_type=F32)


def _resident(shape):
    nd = len(shape)
    return pl.BlockSpec(shape, lambda *_: (0,) * nd, pipeline_mode=pl.Buffered(1))


def _stage_chunk_rows(n_rows, n_cols):
    rows = 1 << ((STAGE_SLOT_ELEMS // n_cols).bit_length() - 1)
    rows = min(rows, n_rows)
    while n_rows % rows:
        rows //= 2
    return rows


class _Staged:
    def __init__(self, array, prefix=()):
        self.array, self.prefix = array, tuple(prefix)
        self.rows, self.cols = array.shape[len(self.prefix):]
        self.chunk = _stage_chunk_rows(self.rows, self.cols)
        self.slots = min(STAGE_SLOTS, self.rows // self.chunk)

    def scratch(self):
        return [pltpu.VMEM((self.rows, self.cols), BF16),
                pltpu.VMEM((self.slots, self.chunk, self.cols), self.array.dtype),
                pltpu.SemaphoreType.DMA((self.slots,))]

    def load(self, hbm_ref, vmem_ref, stage_ref, sem_ref):
        src = hbm_ref.at[self.prefix] if self.prefix else hbm_ref
        n_chunks = self.rows // self.chunk
        n_slots = self.slots
        ahead = max(n_slots - 1, 1)

        def copy(c):
            slot = c % n_slots
            return pltpu.make_async_copy(src.at[pl.ds(c * self.chunk, self.chunk)],
                                         stage_ref.at[slot], sem_ref.at[slot])

        for c in range(min(ahead, n_chunks)):
            copy(c).start()
        for c in range(n_chunks):
            if c + ahead < n_chunks:
                copy(c + ahead).start()
            copy(c).wait()
            vmem_ref[c * self.chunk:(c + 1) * self.chunk, :] = (
                stage_ref[c % n_slots].astype(BF16))


def _load_staged(staged, hbm_refs, scratch, first_step):
    groups = [scratch[3 * k:3 * k + 3] for k in range(len(staged))]

    @pl.when(first_step)
    def _():
        for w, hbm_ref, (vmem_ref, stage_ref, sem_ref) in zip(staged, hbm_refs, groups):
            w.load(hbm_ref, vmem_ref, stage_ref, sem_ref)

    return [g[0] for g in groups]


def _staged_specs(staged):
    in_specs = [pl.BlockSpec(memory_space=pl.ANY) for _ in staged]
    scratch = [s for w in staged for s in w.scratch()]
    return in_specs, [w.array for w in staged], scratch


def _gain_spec_d(row, d):
    return pl.BlockSpec((None, 1, d), lambda *_: (row, 0, 0))


def _params(n_grid):
    return pltpu.CompilerParams(
        dimension_semantics=("arbitrary",) * n_grid,
        vmem_limit_bytes=VMEM_LIMIT_BYTES,
    )


def _ffn_kernel(x_ref, gpre_ref, gpost_ref, wgu_hbm, wd_hbm, o_ref, *scratch, d_ff, n_sub, staged):
    wgu_ref, wd_ref = _load_staged(staged, (wgu_hbm, wd_hbm), scratch, pl.program_id(0) == 0)
    ts = x_ref.shape[0] // n_sub
    for sub in range(n_sub):
        rows = slice(sub * ts, (sub + 1) * ts)
        x = x_ref[rows, :]
        xn = _rms(x, gpre_ref[...]).astype(BF16)
        a = _dot(xn, wgu_ref[:, :d_ff])
        b = _dot(xn, wgu_ref[:, d_ff:])
        y = _dot((jax.nn.silu(a) * b).astype(BF16), wd_ref[...])
        o_ref[rows, :] = x + 0.5 * _rms(y, gpost_ref[...])


def _ffn(x2, gains, row_pre, row_post, w_gu, w_down, *, tm, n_sub):
    t, d = x2.shape
    staged = (w_gu, w_down)
    w_specs, w_args, w_scratch = _staged_specs(staged)
    tile = pl.BlockSpec((tm, d), lambda i: (i, 0))
    return pl.pallas_call(
        functools.partial(_ffn_kernel, d_ff=w_down.rows, n_sub=n_sub, staged=staged),
        grid=(t // tm,),
        in_specs=[tile, _gain_spec_d(row_pre, d), _gain_spec_d(row_post, d)] + w_specs,
        out_specs=tile,
        out_shape=jax.ShapeDtypeStruct((t, d), F32),
        scratch_shapes=w_scratch,
        compiler_params=_params(1),
        name="ffn",
    )(x2, gains, gains, *w_args)


def _ple_kernel(x_ref, p_ref, g_ref, wg_hbm, wp_hbm, o_ref, *scratch, n_sub, staged):
    wg_ref, wp_ref = _load_staged(staged, (wg_hbm, wp_hbm), scratch, pl.program_id(0) == 0)
    ts = x_ref.shape[0] // n_sub
    for sub in range(n_sub):
        rows = slice(sub * ts, (sub + 1) * ts)
        x = x_ref[rows, :]
        xn = _rms(x, g_ref[...]).astype(BF16)
        gate = jax.nn.sigmoid(_dot(xn, wg_ref[...]))
        proj = _dot(p_ref[rows, :].astype(BF16), wp_ref[...])
        o_ref[rows, :] = x + gate * proj


def _ple(x2, p3, layer, gains, row, w_gate, w_proj, *, tm, n_sub):
    t, d = x2.shape
    pd = p3.shape[-1]
    staged = (w_gate, w_proj)
    w_specs, w_args, w_scratch = _staged_specs(staged)
    tile = pl.BlockSpec((tm, d), lambda i: (i, 0))
    return pl.pallas_call(
        functools.partial(_ple_kernel, n_sub=n_sub, staged=staged),
        grid=(t // tm,),
        in_specs=[tile, pl.BlockSpec((None, tm, pd), lambda i: (layer, i, 0)),
                  _gain_spec_d(row, d)] + w_specs,
        out_specs=tile,
        out_shape=jax.ShapeDtypeStruct((t, d), F32),
        scratch_shapes=w_scratch,
        compiler_params=_params(1),
        name="ple",
    )(x2, p3, gains, *w_args)


def _first_step_2d():
    return (pl.program_id(0) == 0) & (pl.program_id(1) == 0)


def _conv_kernel(x_ref, gpre_ref, gpost_ref, cw_ref, win_hbm, wout_hbm, o_ref, u_buf, *scratch,
                 n_sub, staged):
    win_ref, wout_ref = _load_staged(staged, (win_hbm, wout_hbm), scratch, _first_step_2d())
    tm, d = x_ref.shape
    ts = tm // n_sub

    @pl.when(pl.program_id(1) == 0)
    def _():
        u_buf[0:SUBLANES, :] = jnp.zeros((SUBLANES, d), F32)

    def in_proj(sub):
        h = _rms(x_ref[sub * ts:(sub + 1) * ts, :], gpre_ref[...]).astype(BF16)
        return tuple(_dot(h, win_ref[:, k * d:(k + 1) * d]) for k in range(3))

    cw = cw_ref[...]
    u = None
    pieces = in_proj(0)
    for sub in range(n_sub):
        r0 = sub * ts
        bgate, cgate, v = pieces
        u = cgate * v
        u_buf[SUBLANES + r0:SUBLANES + r0 + ts, :] = u
        conv = (cw[2:3, :] * u
                + cw[1:2, :] * u_buf[SUBLANES - 1 + r0:SUBLANES - 1 + r0 + ts, :]
                + cw[0:1, :] * u_buf[SUBLANES - 2 + r0:SUBLANES - 2 + r0 + ts, :])
        if sub + 1 < n_sub:
            pieces = in_proj(sub + 1)
        y = _dot((bgate * conv).astype(BF16), wout_ref[...])
        o_ref[r0:r0 + ts, :] = x_ref[r0:r0 + ts, :] + _rms(y, gpost_ref[...])
    u_buf[0:SUBLANES, :] = u[ts - SUBLANES:, :]


def _conv_mixer(x3, gains, row_pre, row_post, w_in, conv_w, w_out, *, tm, n_sub):
    b, s, d = x3.shape
    staged = (w_in, w_out)
    w_specs, w_args, w_scratch = _staged_specs(staged)
    tile = pl.BlockSpec((None, tm, d), lambda i, j: (i, j, 0))
    return pl.pallas_call(
        functools.partial(_conv_kernel, n_sub=n_sub, staged=staged),
        grid=(b, s // tm),
        in_specs=[tile, _gain_spec_d(row_pre, d), _gain_spec_d(row_post, d),
                  _resident(conv_w.shape)] + w_specs,
        out_specs=tile,
        out_shape=jax.ShapeDtypeStruct((b, s, d), F32),
        scratch_shapes=[pltpu.VMEM((tm + SUBLANES, d), F32)] + w_scratch,
        compiler_params=_params(2),
        name="conv_mixer",
    )(x3, gains, gains, conv_w, *w_args)


def _rel_bucket(dist):
    max_exact = REL_BUCKETS // 2
    dd = jnp.maximum(dist, 1).astype(F32)
    large = max_exact + (jnp.log(dd / max_exact) / math.log(REL_MAX_DIST / max_exact)
                         * (REL_BUCKETS - max_exact)).astype(jnp.int32)
    large = jnp.minimum(large, REL_BUCKETS - 1)
    return jnp.where(dist < max_exact, dist, large)


def _bias_kernel(rb_ref, bucket_ref, o_ref):
    bucket = bucket_ref[...]
    qi = lax.broadcasted_iota(jnp.int32, bucket.shape, 0)
    kj = lax.broadcasted_iota(jnp.int32, bucket.shape, 1)
    causal = kj <= qi
    hits = [bucket == bkt for bkt in range(REL_BUCKETS)]
    neg = jnp.full(bucket.shape, -jnp.inf, F32)
    for h in range(ATT_HEADS):
        acc = jnp.zeros(bucket.shape, F32)
        for bkt in range(REL_BUCKETS):
            acc = jnp.where(hits[bkt], rb_ref[bkt, h], acc)
        o_ref[0, h] = jnp.where(causal, acc, neg)
        o_ref[1, h] = acc


def _bias_tables(rel_bias):
    assert WINDOW == BLOCK
    qi = jnp.arange(BLOCK)[:, None]
    kj = jnp.arange(BLOCK)[None, :]
    dist = jnp.where(kj <= qi, qi - kj, qi + BLOCK - kj)
    bucket = _rel_bucket(dist).astype(jnp.int32)
    out = pl.pallas_call(
        _bias_kernel,
        in_specs=[pl.BlockSpec(memory_space=pltpu.SMEM),
                  pl.BlockSpec(memory_space=pltpu.VMEM)],
        out_specs=pl.BlockSpec(memory_space=pltpu.VMEM),
        out_shape=jax.ShapeDtypeStruct((2, ATT_HEADS, BLOCK, BLOCK), F32),
        name="rel_bias_table",
    )(rel_bias, bucket)
    return out.reshape(2, ATT_KV_HEADS, ATT_GROUP * BLOCK, BLOCK)


def _swa_kernel(x_ref, xprev_ref, gpre_ref, gpost_ref, bias_ref, sink_ref, wqkv_hbm, wo_hbm, o_ref,
                kk_buf, vv_buf, att_buf, *scratch, staged):
    wqkv_ref, wo_ref = _load_staged(staged, (wqkv_hbm, wo_hbm), scratch, _first_step_2d())
    tm, d = x_ref.shape
    n_blk = tm // BLOCK
    dq = ATT_HEADS * ATT_HEAD_DIM
    dkv = ATT_KV_HEADS * ATT_HEAD_DIM
    first_tile = pl.program_id(1) == 0

    x = x_ref[...]
    h = _rms(x, gpre_ref[...]).astype(BF16)
    qkv = _dot(h, wqkv_ref[...])
    q = (qkv[:, :dq] * (ATT_HEAD_DIM ** -0.5)).astype(BF16)
    h_prev = _rms(xprev_ref[...], gpre_ref[...]).astype(BF16)
    kv_prev = _dot(h_prev, wqkv_ref[:, dq:])
    k = jnp.concatenate([kv_prev[:, :dkv], qkv[:, dq:dq + dkv]], axis=0)
    v = jnp.concatenate([kv_prev[:, dkv:], qkv[:, dq + dkv:]], axis=0)

    low = lax.broadcasted_iota(jnp.int32, k.shape, 1) < ATT_HEAD_DIM
    k_sw = pltpu.roll(k, ATT_HEAD_DIM, 1)
    v_sw = pltpu.roll(v, ATT_HEAD_DIM, 1)
    kk_buf[0] = jnp.where(low, k, k_sw).astype(BF16)
    kk_buf[1] = jnp.where(low, k_sw, k).astype(BF16)
    vv_buf[0, :, 0:LANES] = jnp.where(low, v, v_sw).astype(BF16)
    vv_buf[1, :, 0:LANES] = jnp.where(low, v_sw, v).astype(BF16)
    vv_buf[:, :, LANES:] = jnp.ones((ATT_KV_HEADS, tm + BLOCK, LANES), BF16)

    low_b = lax.broadcasted_iota(jnp.int32, (BLOCK, LANES), 1) < ATT_HEAD_DIM
    zero = jnp.zeros((BLOCK, LANES), BF16)
    pairs = ATT_GROUP // 2
    stacked = (2 * BLOCK, BLOCK)
    q_pos = lax.broadcasted_iota(jnp.int32, stacked, 0) & (BLOCK - 1)
    causal = lax.broadcasted_iota(jnp.int32, stacked, 1) <= q_pos
    zero_p = jnp.zeros(stacked, BF16)
    for n in range(n_blk):
        rows = slice(n * BLOCK, (n + 1) * BLOCK)
        variant = jnp.where(first_tile, 0, 1) if n == 0 else 1
        for hk in range(ATT_KV_HEADS):
            kb = kk_buf[hk, n * BLOCK:(n + 2) * BLOCK, :]
            vb = vv_buf[hk, n * BLOCK:(n + 2) * BLOCK, :]
            for j in range(pairs):
                col = (hk * pairs + j) * LANES
                pair_rows = slice(2 * j * BLOCK, (2 * j + 2) * BLOCK)
                q2 = q[rows, col:col + LANES]
                qs = jnp.concatenate([jnp.where(low_b, q2, zero), jnp.where(low_b, zero, q2)], axis=0)
                s2 = _dot_nt(qs, kb)
                logits = (jnp.where(causal, s2[:, BLOCK:], s2[:, :BLOCK])
                          + bias_ref[variant, hk, pair_rows, :])
                sink = sink_ref[hk, pair_rows, :]
                m = jnp.maximum(jnp.max(logits, axis=-1, keepdims=True), sink)
                e = jnp.exp(logits - m).astype(BF16)
                e2 = jnp.concatenate([jnp.where(causal, zero_p, e),
                                      jnp.where(causal, e, zero_p)], axis=1)
                r = _dot(e2, vb)
                out = r[:, :LANES] / (r[:, LANES:] + jnp.exp(sink - m))
                att_buf[rows, col:col + LANES] = jnp.where(low_b, out[:BLOCK], out[BLOCK:]).astype(BF16)

    y = _dot(att_buf[...], wo_ref[...])
    o_ref[...] = x + _rms(y, gpost_ref[...])


def _swa_mixer(x3, gains, row_pre, row_post, w_qkv, bias_tab, sinks, w_o, *, tm):
    b, s, d = x3.shape
    blocks_per_tile = tm // BLOCK
    tile = pl.BlockSpec((None, tm, d), lambda i, j: (i, j, 0))
    prev_block = pl.BlockSpec((None, BLOCK, d),
                              lambda i, j: (i, jnp.maximum(j * blocks_per_tile - 1, 0), 0))
    sink_col = jnp.repeat(sinks.astype(F32).reshape(ATT_KV_HEADS, ATT_GROUP), BLOCK, axis=1)
    sink_col = jnp.broadcast_to(sink_col[..., None], sink_col.shape + (LANES,))
    staged = (w_qkv, w_o)
    w_specs, w_args, w_scratch = _staged_specs(staged)
    return pl.pallas_call(
        functools.partial(_swa_kernel, staged=staged),
        grid=(b, s // tm),
        in_specs=[tile, prev_block, _gain_spec_d(row_pre, d), _gain_spec_d(row_post, d),
                  _resident(bias_tab.shape), _resident(sink_col.shape)] + w_specs,
        out_specs=tile,
        out_shape=jax.ShapeDtypeStruct((b, s, d), F32),
        scratch_shapes=[pltpu.VMEM((ATT_KV_HEADS, tm + BLOCK, LANES), BF16),
                        pltpu.VMEM((ATT_KV_HEADS, tm + BLOCK, 2 * LANES), BF16),
                        pltpu.VMEM((tm, ATT_HEADS * ATT_HEAD_DIM), BF16)] + w_scratch,
        compiler_params=_params(2),
        name="swa_mixer",
    )(x3, x3, gains, gains, bias_tab, sink_col, *w_args)


def _ret_tables(s, dk):
    f32 = F32
    half = dk // 2
    inv = ROPE_BASE ** (-jnp.arange(half, dtype=f32) / half)
    ang = jnp.arange(s)[:, None].astype(f32) * inv[None, :]
    c = RET_CHUNK
    log_g = jnp.log(1.0 - 2.0 ** (-5.0 - jnp.arange(RET_HEADS, dtype=f32)))
    idx = jnp.arange(c, dtype=f32)
    diff = idx[:, None] - idx[None, :]
    decay_mask = jnp.where(diff >= 0, jnp.exp(log_g[:, None, None] * jnp.maximum(diff, 0.0)), 0.0)
    q_decay = jnp.exp(log_g[:, None] * (idx + 1.0))[..., None]
    k_decay = jnp.exp(log_g[:, None] * (c - 1.0 - idx))[..., None]
    chunk_decay = jnp.exp(log_g * c)
    return jnp.cos(ang), jnp.sin(ang), decay_mask, q_decay, k_decay, chunk_decay


def _ret_kernel(x_ref, gpre_ref, gpost_ref, cos_ref, sin_ref, dmask_ref, qdec_ref, kdec_ref,
                cdec_ref, w_hbm, wo_hbm, o_ref, state_ref, gated_buf, *scratch,
                dk, dv, n_sub, staged):
    w_ref, wo_ref = _load_staged(staged, (w_hbm, wo_hbm), scratch, _first_step_2d())
    tm, d = x_ref.shape
    ts = tm // n_sub
    n_chunk = ts // RET_CHUNK
    half = dk // 2
    hq = RET_HEADS * dk

    @pl.when(pl.program_id(1) == 0)
    def _():
        state_ref[...] = jnp.zeros(state_ref.shape, F32)

    def normed(sub):
        return _rms(x_ref[sub * ts:(sub + 1) * ts, :], gpre_ref[...]).astype(BF16)

    def head_proj(h, hd):
        return tuple(_dot(h, w_ref[:, c0:c0 + width]) for c0, width in (
            (hd * dk, dk), (hq + hd * dk, dk), (2 * hq + hd * dv, dv),
            (2 * hq + RET_HEADS * dv + hd * dv, dv)))

    pieces = [head_proj(normed(0), hd) for hd in range(RET_HEADS)]
    for sub in range(n_sub):
        r0 = sub * ts
        x = x_ref[r0:r0 + ts, :]
        cos = cos_ref[r0:r0 + ts, :]
        sin = sin_ref[r0:r0 + ts, :]
        h_next = normed(sub + 1) if sub + 1 < n_sub else None
        next_pieces = []

        def rot(t):
            t1, t2 = t[:, :half], t[:, half:]
            return jnp.concatenate([t1 * cos - t2 * sin, t1 * sin + t2 * cos], axis=-1)

        for hd in range(RET_HEADS):
            q_raw, k_raw, v, g = pieces[hd]
            q = rot(q_raw)
            k = rot(k_raw) * (dk ** -0.5)
            if h_next is not None:
                next_pieces.append(head_proj(h_next, hd))
            dmask = dmask_ref[hd]
            qdec = qdec_ref[hd]
            kdec = kdec_ref[hd]
            cdec = cdec_ref[hd]
            state = state_ref[hd]
            for c in range(n_chunk):
                rows = slice(c * RET_CHUNK, (c + 1) * RET_CHUNK)
                qc = q[rows].astype(BF16)
                kc = k[rows]
                vc = v[rows].astype(BF16)
                inner = _dot_nt(qc, kc.astype(BF16)) * dmask
                o = _dot(inner.astype(BF16), vc) + _dot(qc, state.astype(BF16)) * qdec
                state = state * cdec + _dot_tn((kc * kdec).astype(BF16), vc)
                mu = jnp.mean(o, axis=-1, keepdims=True)
                oc = o - mu
                var = jnp.mean(oc * oc, axis=-1, keepdims=True)
                on = oc * lax.rsqrt(var + NORM_EPS)
                gated_buf[r0 + c * RET_CHUNK:r0 + (c + 1) * RET_CHUNK, hd * dv:(hd + 1) * dv] = (
                    jax.nn.silu(g[rows]) * on).astype(BF16)
            state_ref[hd] = state

        pieces = next_pieces
        y = _dot(gated_buf[r0:r0 + ts, :], wo_ref[...])
        o_ref[r0:r0 + ts, :] = x + _rms(y, gpost_ref[...])


def _ret_mixer(x3, gains, row_pre, row_post, w_qkvg, w_o, *, tm, n_sub):
    b, s, d = x3.shape
    dk = d // RET_HEADS
    dv = 2 * dk
    cos, sin, dmask, qdec, kdec, cdec = _ret_tables(s, dk)
    cdec = cdec.reshape(RET_HEADS, 1, 1)
    tile = pl.BlockSpec((None, tm, d), lambda i, j: (i, j, 0))
    rope = pl.BlockSpec((tm, dk // 2), lambda i, j: (j, 0))
    staged = (w_qkvg, w_o)
    w_specs, w_args, w_scratch = _staged_specs(staged)
    return pl.pallas_call(
        functools.partial(_ret_kernel, dk=dk, dv=dv, n_sub=n_sub, staged=staged),
        grid=(b, s // tm),
        in_specs=[tile, _gain_spec_d(row_pre, d), _gain_spec_d(row_post, d),
                  rope, rope, _resident(dmask.shape),
                  _resident(qdec.shape), _resident(kdec.shape), _resident(cdec.shape)] + w_specs,
        out_specs=tile,
        out_shape=jax.ShapeDtypeStruct((b, s, d), F32),
        scratch_shapes=[pltpu.VMEM((RET_HEADS, dk, dv), F32),
                        pltpu.VMEM((tm, RET_HEADS * dv), BF16)] + w_scratch,
        compiler_params=_params(2),
        name="ret_mixer",
    )(x3, gains, gains, cos, sin, dmask, qdec, kdec, cdec, *w_args)


def _tile(n, want):
    t = min(n, want)
    assert n % t == 0, (n, t)
    return t


def kernel(x, p, norm_g, ffn_w_gu, ffn_w_down, ple_w_proj, ple_w_gate, rel_bias, conv_w_in, conv_w,
           conv_w_out, swa_w_qkv, swa_sinks, swa_w_o, ret_w_qkvg, ret_w_o):
    depth = norm_g.shape[0]
    b, s, d = x.shape
    t = b * s
    gains = norm_g.astype(F32).reshape(depth * N_NORMS, 1, d)
    p3 = p.reshape(depth, t, p.shape[-1])
    bias_tab = _bias_tables(rel_bias.astype(F32)) if depth > 1 else None

    tm_ffn = _tile(t, 1024)
    tm_ple = _tile(t, 1024)
    tm_conv = _tile(s, 1024)
    tm_swa = _tile(s, 512)
    tm_ret = _tile(s, 1024)

    for i in range(depth):
        kind, j = i % N_MIXERS, i // N_MIXERS
        r = i * N_NORMS
        x2 = _ffn(x.reshape(t, d), gains, r + 0, r + 1,
                  _Staged(ffn_w_gu, (i, 0)), _Staged(ffn_w_down, (i, 0)),
                  tm=tm_ffn, n_sub=4)
        x = x2.reshape(b, s, d)
        if kind == 0:
            x = _conv_mixer(x, gains, r + 2, r + 3, _Staged(conv_w_in, (j,)), conv_w[j].astype(F32),
                            _Staged(conv_w_out, (j,)), tm=tm_conv, n_sub=4)
        elif kind == 1:
            x = _swa_mixer(x, gains, r + 2, r + 3, _Staged(swa_w_qkv, (j,)), bias_tab, swa_sinks[j],
                           _Staged(swa_w_o, (j,)), tm=tm_swa)
        else:
            x = _ret_mixer(x, gains, r + 2, r + 3, _Staged(ret_w_qkvg, (j,)), _Staged(ret_w_o, (j,)),
                           tm=tm_ret, n_sub=4)
        x2 = _ffn(x.reshape(t, d), gains, r + 4, r + 5,
                  _Staged(ffn_w_gu, (i, 1)), _Staged(ffn_w_down, (i, 1)),
                  tm=tm_ffn, n_sub=4)
        x2 = _ple(x2, p3, i, gains, r + 6, _Staged(ple_w_gate, (i,)), _Staged(ple_w_proj, (i,)),
                  tm=tm_ple, n_sub=4)
        x = x2.reshape(b, s, d)
    return x
```

```python
import functools
import math

import jax
import jax.numpy as jnp
from jax import lax
from jax.experimental import pallas as pl
from jax.experimental.pallas import tpu as pltpu

N_MIXERS = 3
N_NORMS = 7
NORM_EPS = 1e-6
CONV_WIDTH = 3
ATT_HEADS = 16
ATT_KV_HEADS = 2
ATT_HEAD_DIM = 64
ATT_GROUP = ATT_HEADS // ATT_KV_HEADS
WINDOW = 128
BLOCK = 128
REL_BUCKETS = 32
REL_MAX_DIST = 128
RET_HEADS = 4
RET_CHUNK = 256
ROPE_BASE = 10000.0

LANES = 128
SUBLANES = 8
VMEM_LIMIT_BYTES = 56 * 1024 * 1024
STAGE_SLOT_ELEMS = 384 * 1024
STAGE_SLOTS = 4

F32 = jnp.float32
BF16 = jnp.bfloat16


def _rms(x, g):
    ms = jnp.mean(x * x, axis=-1, keepdims=True)
    return x * lax.rsqrt(ms + NORM_EPS) * g


def _dot(a, b):
    return jnp.dot(a, b, preferred_element_type=F32)


def _dot_nt(a, b):
    return lax.dot_general(a, b, (((1,), (1,)), ((), ())), preferred_element_type=F32)


def _dot_tn(a, b):
    return lax.dot_general(a, b, (((0,), (0,)), ((), ())), preferred_element_type=F32)


def _resident(shape):
    nd = len(shape)
    return pl.BlockSpec(shape, lambda *_: (0,) * nd, pipeline_mode=pl.Buffered(1))


def _stage_chunk_rows(n_rows, n_cols):
    rows = 1 << ((STAGE_SLOT_ELEMS // n_cols).bit_length() - 1)
    rows = min(rows, n_rows)
    while n_rows % rows:
        rows //= 2
    return rows


class _Staged:
    def __init__(self, array, prefix=()):
        self.array, self.prefix = array, tuple(prefix)
        self.rows, self.cols = array.shape[len(self.prefix):]
        self.chunk = _stage_chunk_rows(self.rows, self.cols)
        self.slots = min(STAGE_SLOTS, self.rows // self.chunk)

    def scratch(self):
        return [pltpu.VMEM((self.rows, self.cols), BF16),
                pltpu.VMEM((self.slots, self.chunk, self.cols), self.array.dtype),
                pltpu.SemaphoreType.DMA((self.slots,))]

    def load(self, hbm_ref, vmem_ref, stage_ref, sem_ref):
        src = hbm_ref.at[self.prefix] if self.prefix else hbm_ref
        n_chunks = self.rows // self.chunk
        n_slots = self.slots
        ahead = max(n_slots - 1, 1)

        def copy(c):
            slot = c % n_slots
            return pltpu.make_async_copy(src.at[pl.ds(c * self.chunk, self.chunk)],
                                         stage_ref.at[slot], sem_ref.at[slot])

        for c in range(min(ahead, n_chunks)):
            copy(c).start()
        for c in range(n_chunks):
            if c + ahead < n_chunks:
                copy(c + ahead).start()
            copy(c).wait()
            vmem_ref[c * self.chunk:(c + 1) * self.chunk, :] = (
                stage_ref[c % n_slots].astype(BF16))


def _load_staged(staged, hbm_refs, scratch, first_step):
    groups = [scratch[3 * k:3 * k + 3] for k in range(len(staged))]

    @pl.when(first_step)
    def _():
        for w, hbm_ref, (vmem_ref, stage_ref, sem_ref) in zip(staged, hbm_refs, groups):
            w.load(hbm_ref, vmem_ref, stage_ref, sem_ref)

    return [g[0] for g in groups]


def _staged_specs(staged):
    in_specs = [pl.BlockSpec(memory_space=pl.ANY) for _ in staged]
    scratch = [s for w in staged for s in w.scratch()]
    return in_specs, [w.array for w in staged], scratch


def _gain_spec_d(row, d):
    return pl.BlockSpec((None, 1, d), lambda *_: (row, 0, 0))


def _params(n_grid):
    return pltpu.CompilerParams(
        dimension_semantics=("arbitrary",) * n_grid,
        vmem_limit_bytes=VMEM_LIMIT_BYTES,
    )


def _ffn_kernel(x_ref, gpre_ref, gpost_ref, wgu_hbm, wd_hbm, o_ref, *scratch, d_ff, n_sub, staged):
    wgu_ref, wd_ref = _load_staged(staged, (wgu_hbm, wd_hbm), scratch, pl.program_id(0) == 0)
    ts = x_ref.shape[0] // n_sub
    for sub in range(n_sub):
        rows = slice(sub * ts, (sub + 1) * ts)
        x = x_ref[rows, :]
        xn = _rms(x, gpre_ref[...]).astype(BF16)
        a = _dot(xn, wgu_ref[:, :d_ff])
        b = _dot(xn, wgu_ref[:, d_ff:])
        y = _dot((jax.nn.silu(a) * b).astype(BF16), wd_ref[...])
        o_ref[rows, :] = x + 0.5 * _rms(y, gpost_ref[...])


def _ffn(x2, gains, row_pre, row_post, w_gu, w_down, *, tm, n_sub):
    t, d = x2.shape
    staged = (w_gu, w_down)
    w_specs, w_args, w_scratch = _staged_specs(staged)
    tile = pl.BlockSpec((tm, d), lambda i: (i, 0))
    return pl.pallas_call(
        functools.partial(_ffn_kernel, d_ff=w_down.rows, n_sub=n_sub, staged=staged),
        grid=(t // tm,),
        in_specs=[tile, _gain_spec_d(row_pre, d), _gain_spec_d(row_post, d)] + w_specs,
        out_specs=tile,
        out_shape=jax.ShapeDtypeStruct((t, d), F32),
        scratch_shapes=w_scratch,
        compiler_params=_params(1),
        name="ffn",
    )(x2, gains, gains, *w_args)


def _ple_kernel(x_ref, p_ref, g_ref, wg_hbm, wp_hbm, o_ref, *scratch, n_sub, staged):
    wg_ref, wp_ref = _load_staged(staged, (wg_hbm, wp_hbm), scratch, pl.program_id(0) == 0)
    ts = x_ref.shape[0] // n_sub
    for sub in range(n_sub):
        rows = slice(sub * ts, (sub + 1) * ts)
        x = x_ref[rows, :]
        xn = _rms(x, g_ref[...]).astype(BF16)
        gate = jax.nn.sigmoid(_dot(xn, wg_ref[...]))
        proj = _dot(p_ref[rows, :].astype(BF16), wp_ref[...])
        o_ref[rows, :] = x + gate * proj


def _ple(x2, p3, layer, gains, row, w_gate, w_proj, *, tm, n_sub):
    t, d = x2.shape
    pd = p3.shape[-1]
    staged = (w_gate, w_proj)
    w_specs, w_args, w_scratch = _staged_specs(staged)
    tile = pl.BlockSpec((tm, d), lambda i: (i, 0))
    return pl.pallas_call(
        functools.partial(_ple_kernel, n_sub=n_sub, staged=staged),
        grid=(t // tm,),
        in_specs=[tile, pl.BlockSpec((None, tm, pd), lambda i: (layer, i, 0)),
                  _gain_spec_d(row, d)] + w_specs,
        out_specs=tile,
        out_shape=jax.ShapeDtypeStruct((t, d), F32),
        scratch_shapes=w_scratch,
        compiler_params=_params(1),
        name="ple",
    )(x2, p3, gains, *w_args)


def _first_step_2d():
    return (pl.program_id(0) == 0) & (pl.program_id(1) == 0)


def _conv_kernel(x_ref, gpre_ref, gpost_ref, cw_ref, win_hbm, wout_hbm, o_ref, u_buf, *scratch,
                 n_sub, staged):
    win_ref, wout_ref = _load_staged(staged, (win_hbm, wout_hbm), scratch, _first_step_2d())
    tm, d = x_ref.shape
    ts = tm // n_sub

    @pl.when(pl.program_id(1) == 0)
    def _():
        u_buf[0:SUBLANES, :] = jnp.zeros((SUBLANES, d), F32)

    def in_proj(sub):
        h = _rms(x_ref[sub * ts:(sub + 1) * ts, :], gpre_ref[...]).astype(BF16)
        return tuple(_dot(h, win_ref[:, k * d:(k + 1) * d]) for k in range(3))

    cw = cw_ref[...]
    u = None
    pieces = in_proj(0)
    for sub in range(n_sub):
        r0 = sub * ts
        bgate, cgate, v = pieces
        u = cgate * v
        u_buf[SUBLANES + r0:SUBLANES + r0 + ts, :] = u
        conv = (cw[2:3, :] * u
                + cw[1:2, :] * u_buf[SUBLANES - 1 + r0:SUBLANES - 1 + r0 + ts, :]
                + cw[0:1, :] * u_buf[SUBLANES - 2 + r0:SUBLANES - 2 + r0 + ts, :])
        if sub + 1 < n_sub:
            pieces = in_proj(sub + 1)
        y = _dot((bgate * conv).astype(BF16), wout_ref[...])
        o_ref[r0:r0 + ts, :] = x_ref[r0:r0 + ts, :] + _rms(y, gpost_ref[...])
    u_buf[0:SUBLANES, :] = u[ts - SUBLANES:, :]


def _conv_mixer(x3, gains, row_pre, row_post, w_in, conv_w, w_out, *, tm, n_sub):
    b, s, d = x3.shape
    staged = (w_in, w_out)
    w_specs, w_args, w_scratch = _staged_specs(staged)
    tile = pl.BlockSpec((None, tm, d), lambda i, j: (i, j, 0))
    return pl.pallas_call(
        functools.partial(_conv_kernel, n_sub=n_sub, staged=staged),
        grid=(b, s // tm),
        in_specs=[tile, _gain_spec_d(row_pre, d), _gain_spec_d(row_post, d),
                  _resident(conv_w.shape)] + w_specs,
        out_specs=tile,
        out_shape=jax.ShapeDtypeStruct((b, s, d), F32),
        scratch_shapes=[pltpu.VMEM((tm + SUBLANES, d), F32)] + w_scratch,
        compiler_params=_params(2),
        name="conv_mixer",
    )(x3, gains, gains, conv_w, *w_args)


def _rel_bucket(dist):
    max_exact = REL_BUCKETS // 2
    dd = jnp.maximum(dist, 1).astype(F32)
    large = max_exact + (jnp.log(dd / max_exact) / math.log(REL_MAX_DIST / max_exact)
                         * (REL_BUCKETS - max_exact)).astype(jnp.int32)
    large = jnp.minimum(large, REL_BUCKETS - 1)
    return jnp.where(dist < max_exact, dist, large)


def _bias_kernel(rb_ref, bucket_ref, o_ref):
    bucket = bucket_ref[...]
    qi = lax.broadcasted_iota(jnp.int32, bucket.shape, 0)
    kj = lax.broadcasted_iota(jnp.int32, bucket.shape, 1)
    causal = kj <= qi
    hits = [bucket == bkt for bkt in range(REL_BUCKETS)]
    neg = jnp.full(bucket.shape, -jnp.inf, F32)
    for h in range(ATT_HEADS):
        acc = jnp.zeros(bucket.shape, F32)
        for bkt in range(REL_BUCKETS):
            acc = jnp.where(hits[bkt], rb_ref[bkt, h], acc)
        o_ref[0, h] = jnp.where(causal, acc, neg)
        o_ref[1, h] = acc


def _bias_tables(rel_bias):
    assert WINDOW == BLOCK
    qi = jnp.arange(BLOCK)[:, None]
    kj = jnp.arange(BLOCK)[None, :]
    dist = jnp.where(kj <= qi, qi - kj, qi + BLOCK - kj)
    bucket = _rel_bucket(dist).astype(jnp.int32)
    out = pl.pallas_call(
        _bias_kernel,
        in_specs=[pl.BlockSpec(memory_space=pltpu.SMEM),
                  pl.BlockSpec(memory_space=pltpu.VMEM)],
        out_specs=pl.BlockSpec(memory_space=pltpu.VMEM),
        out_shape=jax.ShapeDtypeStruct((2, ATT_HEADS, BLOCK, BLOCK), F32),
        name="rel_bias_table",
    )(rel_bias, bucket)
    return out.reshape(2, ATT_KV_HEADS, ATT_GROUP * BLOCK, BLOCK)


def _swa_kernel(x_ref, xprev_ref, gpre_ref, gpost_ref, bias_ref, sink_ref, wqkv_hbm, wo_hbm, o_ref,
                kk_buf, vv_buf, att_buf, *scratch, staged):
    wqkv_ref, wo_ref = _load_staged(staged, (wqkv_hbm, wo_hbm), scratch, _first_step_2d())
    tm, d = x_ref.shape
    n_blk = tm // BLOCK
    dq = ATT_HEADS * ATT_HEAD_DIM
    dkv = ATT_KV_HEADS * ATT_HEAD_DIM
    first_tile = pl.program_id(1) == 0

    x = x_ref[...]
    h = _rms(x, gpre_ref[...]).astype(BF16)
    qkv = _dot(h, wqkv_ref[...])
    q = (qkv[:, :dq] * (ATT_HEAD_DIM ** -0.5)).astype(BF16)
    h_prev = _rms(xprev_ref[...], gpre_ref[...]).astype(BF16)
    kv_prev = _dot(h_prev, wqkv_ref[:, dq:])
    k = jnp.concatenate([kv_prev[:, :dkv], qkv[:, dq:dq + dkv]], axis=0)
    v = jnp.concatenate([kv_prev[:, dkv:], qkv[:, dq + dkv:]], axis=0)

    low = lax.broadcasted_iota(jnp.int32, k.shape, 1) < ATT_HEAD_DIM
    k_sw = pltpu.roll(k, ATT_HEAD_DIM, 1)
    v_sw = pltpu.roll(v, ATT_HEAD_DIM, 1)
    kk_buf[0] = jnp.where(low, k, k_sw).astype(BF16)
    kk_buf[1] = jnp.where(low, k_sw, k).astype(BF16)
    vv_buf[0, :, 0:LANES] = jnp.where(low, v, v_sw).astype(BF16)
    vv_buf[1, :, 0:LANES] = jnp.where(low, v_sw, v).astype(BF16)
    vv_buf[:, :, LANES:] = jnp.ones((ATT_KV_HEADS, tm + BLOCK, LANES), BF16)

    low_b = lax.broadcasted_iota(jnp.int32, (BLOCK, LANES), 1) < ATT_HEAD_DIM
    zero = jnp.zeros((BLOCK, LANES), BF16)
    pairs = ATT_GROUP // 2
    stacked = (2 * BLOCK, BLOCK)
    q_pos = lax.broadcasted_iota(jnp.int32, stacked, 0) & (BLOCK - 1)
    causal = lax.broadcasted_iota(jnp.int32, stacked, 1) <= q_pos
    zero_p = jnp.zeros(stacked, BF16)
    for n in range(n_blk):
        rows = slice(n * BLOCK, (n + 1) * BLOCK)
        variant = jnp.where(first_tile, 0, 1) if n == 0 else 1
        for hk in range(ATT_KV_HEADS):
            kb = kk_buf[hk, n * BLOCK:(n + 2) * BLOCK, :]
            vb = vv_buf[hk, n * BLOCK:(n + 2) * BLOCK, :]
            for j in range(pairs):
                col = (hk * pairs + j) * LANES
                pair_rows = slice(2 * j * BLOCK, (2 * j + 2) * BLOCK)
                q2 = q[rows, col:col + LANES]
                qs = jnp.concatenate([jnp.where(low_b, q2, zero), jnp.where(low_b, zero, q2)], axis=0)
                s2 = _dot_nt(qs, kb)
                logits = (jnp.where(causal, s2[:, BLOCK:], s2[:, :BLOCK])
                          + bias_ref[variant, hk, pair_rows, :])
                sink = sink_ref[hk, pair_rows, :]
                m = jnp.maximum(jnp.max(logits, axis=-1, keepdims=True), sink)
                e = jnp.exp(logits - m).astype(BF16)
                e2 = jnp.concatenate([jnp.where(causal, zero_p, e),
                                      jnp.where(causal, e, zero_p)], axis=1)
                r = _dot(e2, vb)
                out = r[:, :LANES] / (r[:, LANES:] + jnp.exp(sink - m))
                att_buf[rows, col:col + LANES] = jnp.where(low_b, out[:BLOCK], out[BLOCK:]).astype(BF16)

    y = _dot(att_buf[...], wo_ref[...])
    o_ref[...] = x + _rms(y, gpost_ref[...])


def _swa_mixer(x3, gains, row_pre, row_post, w_qkv, bias_tab, sinks, w_o, *, tm):
    b, s, d = x3.shape
    blocks_per_tile = tm // BLOCK
    tile = pl.BlockSpec((None, tm, d), lambda i, j: (i, j, 0))
    prev_block = pl.BlockSpec((None, BLOCK, d),
                              lambda i, j: (i, jnp.maximum(j * blocks_per_tile - 1, 0), 0))
    sink_col = jnp.repeat(sinks.astype(F32).reshape(ATT_KV_HEADS, ATT_GROUP), BLOCK, axis=1)
    sink_col = jnp.broadcast_to(sink_col[..., None], sink_col.shape + (LANES,))
    staged = (w_qkv, w_o)
    w_specs, w_args, w_scratch = _staged_specs(staged)
    return pl.pallas_call(
        functools.partial(_swa_kernel, staged=staged),
        grid=(b, s // tm),
        in_specs=[tile, prev_block, _gain_spec_d(row_pre, d), _gain_spec_d(row_post, d),
                  _resident(bias_tab.shape), _resident(sink_col.shape)] + w_specs,
        out_specs=tile,
        out_shape=jax.ShapeDtypeStruct((b, s, d), F32),
        scratch_shapes=[pltpu.VMEM((ATT_KV_HEADS, tm + BLOCK, LANES), BF16),
                        pltpu.VMEM((ATT_KV_HEADS, tm + BLOCK, 2 * LANES), BF16),
                        pltpu.VMEM((tm, ATT_HEADS * ATT_HEAD_DIM), BF16)] + w_scratch,
        compiler_params=_params(2),
        name="swa_mixer",
    )(x3, x3, gains, gains, bias_tab, sink_col, *w_args)


def _ret_tables(s, dk):
    f32 = F32
    half = dk // 2
    inv = ROPE_BASE ** (-jnp.arange(half, dtype=f32) / half)
    ang = jnp.arange(s)[:, None].astype(f32) * inv[None, :]
    c = RET_CHUNK
    log_g = jnp.log(1.0 - 2.0 ** (-5.0 - jnp.arange(RET_HEADS, dtype=f32)))
    idx = jnp.arange(c, dtype=f32)
    diff = idx[:, None] - idx[None, :]
    decay_mask = jnp.where(diff >= 0, jnp.exp(log_g[:, None, None] * jnp.maximum(diff, 0.0)), 0.0)
    q_decay = jnp.exp(log_g[:, None] * (idx + 1.0))[..., None]
    k_decay = jnp.exp(log_g[:, None] * (c - 1.0 - idx))[..., None]
    chunk_decay = jnp.exp(log_g * c)
    return jnp.cos(ang), jnp.sin(ang), decay_mask, q_decay, k_decay, chunk_decay


def _ret_kernel(x_ref, gpre_ref, gpost_ref, cos_ref, sin_ref, dmask_ref, qdec_ref, kdec_ref,
                cdec_ref, w_hbm, wo_hbm, o_ref, state_ref, gated_buf, *scratch,
                dk, dv, n_sub, staged):
    w_ref, wo_ref = _load_staged(staged, (w_hbm, wo_hbm), scratch, _first_step_2d())
    tm, d = x_ref.shape
    ts = tm // n_sub
    n_chunk = ts // RET_CHUNK
    half = dk // 2
    hq = RET_HEADS * dk

    @pl.when(pl.program_id(1) == 0)
    def _():
        state_ref[...] = jnp.zeros(state_ref.shape, F32)

    for sub in range(n_sub):
        r0 = sub * ts
        x = x_ref[r0:r0 + ts, :]
        h = _rms(x, gpre_ref[...]).astype(BF16)
        proj = _dot(h, w_ref[...])
        cos = cos_ref[r0:r0 + ts, :]
        sin = sin_ref[r0:r0 + ts, :]

        def rot(t):
            t1, t2 = t[:, :half], t[:, half:]
            return jnp.concatenate([t1 * cos - t2 * sin, t1 * sin + t2 * cos], axis=-1)

        for hd in range(RET_HEADS):
            q = rot(proj[:, hd * dk:(hd + 1) * dk])
            k = rot(proj[:, hq + hd * dk:hq + (hd + 1) * dk]) * (dk ** -0.5)
            v = proj[:, 2 * hq + hd * dv:2 * hq + (hd + 1) * dv]
            g = proj[:, 2 * hq + RET_HEADS * dv + hd * dv:2 * hq + RET_HEADS * dv + (hd + 1) * dv]
            dmask = dmask_ref[hd]
            qdec = qdec_ref[hd]
            kdec = kdec_ref[hd]
            cdec = cdec_ref[hd]
            state = state_ref[hd]
            for c in range(n_chunk):
                rows = slice(c * RET_CHUNK, (c + 1) * RET_CHUNK)
                qc = q[rows].astype(BF16)
                kc = k[rows]
                vc = v[rows].astype(BF16)
                inner = _dot_nt(qc, kc.astype(BF16)) * dmask
                o = _dot(inner.astype(BF16), vc) + _dot(qc, state.astype(BF16)) * qdec
                state = state * cdec + _dot_tn((kc * kdec).astype(BF16), vc)
                mu = jnp.mean(o, axis=-1, keepdims=True)
                oc = o - mu
                var = jnp.mean(oc * oc, axis=-1, keepdims=True)
                on = oc * lax.rsqrt(var + NORM_EPS)
                gated_buf[r0 + c * RET_CHUNK:r0 + (c + 1) * RET_CHUNK, hd * dv:(hd + 1) * dv] = (
                    jax.nn.silu(g[rows]) * on).astype(BF16)
            state_ref[hd] = state

        y = _dot(gated_buf[r0:r0 + ts, :], wo_ref[...])
        o_ref[r0:r0 + ts, :] = x + _rms(y, gpost_ref[...])


def _ret_mixer(x3, gains, row_pre, row_post, w_qkvg, w_o, *, tm, n_sub):
    b, s, d = x3.shape
    dk = d // RET_HEADS
    dv = 2 * dk
    cos, sin, dmask, qdec, kdec, cdec = _ret_tables(s, dk)
    cdec = cdec.reshape(RET_HEADS, 1, 1)
    tile = pl.BlockSpec((None, tm, d), lambda i, j: (i, j, 0))
    rope = pl.BlockSpec((tm, dk // 2), lambda i, j: (j, 0))
    staged = (w_qkvg, w_o)
    w_specs, w_args, w_scratch = _staged_specs(staged)
    return pl.pallas_call(
        functools.partial(_ret_kernel, dk=dk, dv=dv, n_sub=n_sub, staged=staged),
        grid=(b, s // tm),
        in_specs=[tile, _gain_spec_d(row_pre, d), _gain_spec_d(row_post, d),
                  rope, rope, _resident(dmask.shape),
                  _resident(qdec.shape), _resident(kdec.shape), _resident(cdec.shape)] + w_specs,
        out_specs=tile,
        out_shape=jax.ShapeDtypeStruct((b, s, d), F32),
        scratch_shapes=[pltpu.VMEM((RET_HEADS, dk, dv), F32),
                        pltpu.VMEM((tm, RET_HEADS * dv), BF16)] + w_scratch,
        compiler_params=_params(2),
        name="ret_mixer",
    )(x3, gains, gains, cos, sin, dmask, qdec, kdec, cdec, *w_args)


def _tile(n, want):
    t = min(n, want)
    assert n % t == 0, (n, t)
    return t


def kernel(x, p, norm_g, ffn_w_gu, ffn_w_down, ple_w_proj, ple_w_gate, rel_bias, conv_w_in, conv_w,
           conv_w_out, swa_w_qkv, swa_sinks, swa_w_o, ret_w_qkvg, ret_w_o):
    depth = norm_g.shape[0]
    b, s, d = x.shape
    t = b * s
    gains = norm_g.astype(F32).reshape(depth * N_NORMS, 1, d)
    p3 = p.reshape(depth, t, p.shape[-1])
    bias_tab = _bias_tables(rel_bias.astype(F32)) if depth > 1 else None

    tm_ffn = _tile(t, 1024)
    tm_ple = _tile(t, 1024)
    tm_conv = _tile(s, 1024)
    tm_swa = _tile(s, 512)
    tm_ret = _tile(s, 512)

    for i in range(depth):
        kind, j = i % N_MIXERS, i // N_MIXERS
        r = i * N_NORMS
        x2 = _ffn(x.reshape(t, d), gains, r + 0, r + 1,
                  _Staged(ffn_w_gu, (i, 0)), _Staged(ffn_w_down, (i, 0)),
                  tm=tm_ffn, n_sub=4)
        x = x2.reshape(b, s, d)
        if kind == 0:
            x = _conv_mixer(x, gains, r + 2, r + 3, _Staged(conv_w_in, (j,)), conv_w[j].astype(F32),
                            _Staged(conv_w_out, (j,)), tm=tm_conv, n_sub=4)
        elif kind == 1:
            x = _swa_mixer(x, gains, r + 2, r + 3, _Staged(swa_w_qkv, (j,)), bias_tab, swa_sinks[j],
                           _Staged(swa_w_o, (j,)), tm=tm_swa)
        else:
            x = _ret_mixer(x, gains, r + 2, r + 3, _Staged(ret_w_qkvg, (j,)), _Staged(ret_w_o, (j,)),
                           tm=tm_ret, n_sub=2)
        x2 = _ffn(x.reshape(t, d), gains, r + 4, r + 5,
                  _Staged(ffn_w_gu, (i, 1)), _Staged(ffn_w_down, (i, 1)),
                  tm=tm_ffn, n_sub=4)
        x2 = _ple(x2, p3, i, gains, r + 6, _Staged(ple_w_gate, (i,)), _Staged(ple_w_proj, (i,)),
                  tm=tm_ple, n_sub=4)
        x = x2.reshape(b, s, d)
    return x
```

```python
import functools
import math

import jax
import jax.numpy as jnp
from jax import lax
from jax.experimental import pallas as pl
from jax.experimental.pallas import tpu as pltpu

N_MIXERS = 3
N_NORMS = 7
NORM_EPS = 1e-6
CONV_WIDTH = 3
ATT_HEADS = 16
ATT_KV_HEADS = 2
ATT_HEAD_DIM = 64
ATT_GROUP = ATT_HEADS // ATT_KV_HEADS
WINDOW = 128
BLOCK = 128
REL_BUCKETS = 32
REL_MAX_DIST = 128
RET_HEADS = 4
RET_CHUNK = 256
ROPE_BASE = 10000.0

LANES = 128
SUBLANES = 8
VMEM_LIMIT_BYTES = 56 * 1024 * 1024
STAGE_SLOT_ELEMS = 384 * 1024
STAGE_SLOTS = 4

F32 = jnp.float32
BF16 = jnp.bfloat16


def _rms(x, g):
    ms = jnp.mean(x * x, axis=-1, keepdims=True)
    return x * lax.rsqrt(ms + NORM_EPS) * g


def _dot(a, b):
    return jnp.dot(a, b, preferred_element_type=F32)


def _dot_nt(a, b):
    return lax.dot_general(a, b, (((1,), (1,)), ((), ())), preferred_element_type=F32)


def _dot_tn(a, b):
    return lax.dot_general(a, b, (((0,), (0,)), ((), ())), preferred_element_type=F32)


def _resident(shape):
    nd = len(shape)
    return pl.BlockSpec(shape, lambda *_: (0,) * nd, pipeline_mode=pl.Buffered(1))


def _stage_chunk_rows(n_rows, n_cols):
    rows = 1 << ((STAGE_SLOT_ELEMS // n_cols).bit_length() - 1)
    rows = min(rows, n_rows)
    while n_rows % rows:
        rows //= 2
    return rows


class _Staged:
    def __init__(self, array, prefix=()):
        self.array, self.prefix = array, tuple(prefix)
        self.rows, self.cols = array.shape[len(self.prefix):]
        self.chunk = _stage_chunk_rows(self.rows, self.cols)
        self.slots = min(STAGE_SLOTS, self.rows // self.chunk)

    def scratch(self):
        return [pltpu.VMEM((self.rows, self.cols), BF16),
                pltpu.VMEM((self.slots, self.chunk, self.cols), self.array.dtype),
                pltpu.SemaphoreType.DMA((self.slots,))]

    def load(self, hbm_ref, vmem_ref, stage_ref, sem_ref):
        src = hbm_ref.at[self.prefix] if self.prefix else hbm_ref
        n_chunks = self.rows // self.chunk
        n_slots = self.slots
        ahead = max(n_slots - 1, 1)

        def copy(c):
            slot = c % n_slots
            return pltpu.make_async_copy(src.at[pl.ds(c * self.chunk, self.chunk)],
                                         stage_ref.at[slot], sem_ref.at[slot])

        for c in range(min(ahead, n_chunks)):
            copy(c).start()
        for c in range(n_chunks):
            if c + ahead < n_chunks:
                copy(c + ahead).start()
            copy(c).wait()
            vmem_ref[c * self.chunk:(c + 1) * self.chunk, :] = (
                stage_ref[c % n_slots].astype(BF16))


def _load_staged(staged, hbm_refs, scratch, first_step):
    groups = [scratch[3 * k:3 * k + 3] for k in range(len(staged))]

    @pl.when(first_step)
    def _():
        for w, hbm_ref, (vmem_ref, stage_ref, sem_ref) in zip(staged, hbm_refs, groups):
            w.load(hbm_ref, vmem_ref, stage_ref, sem_ref)

    return [g[0] for g in groups]


def _staged_specs(staged):
    in_specs = [pl.BlockSpec(memory_space=pl.ANY) for _ in staged]
    scratch = [s for w in staged for s in w.scratch()]
    return in_specs, [w.array for w in staged], scratch


def _gain_spec_d(row, d):
    return pl.BlockSpec((None, 1, d), lambda *_: (row, 0, 0))


def _params(n_grid):
    return pltpu.CompilerParams(
        dimension_semantics=("arbitrary",) * n_grid,
        vmem_limit_bytes=VMEM_LIMIT_BYTES,
    )


def _ffn_kernel(x_ref, gpre_ref, gpost_ref, wgu_hbm, wd_hbm, o_ref, *scratch, d_ff, n_sub, staged):
    wgu_ref, wd_ref = _load_staged(staged, (wgu_hbm, wd_hbm), scratch, pl.program_id(0) == 0)
    ts = x_ref.shape[0] // n_sub
    for sub in range(n_sub):
        rows = slice(sub * ts, (sub + 1) * ts)
        x = x_ref[rows, :]
        xn = _rms(x, gpre_ref[...]).astype(BF16)
        a = _dot(xn, wgu_ref[:, :d_ff])
        b = _dot(xn, wgu_ref[:, d_ff:])
        y = _dot((jax.nn.silu(a) * b).astype(BF16), wd_ref[...])
        o_ref[rows, :] = x + 0.5 * _rms(y, gpost_ref[...])


def _ffn(x2, gains, row_pre, row_post, w_gu, w_down, *, tm, n_sub):
    t, d = x2.shape
    staged = (w_gu, w_down)
    w_specs, w_args, w_scratch = _staged_specs(staged)
    tile = pl.BlockSpec((tm, d), lambda i: (i, 0))
    return pl.pallas_call(
        functools.partial(_ffn_kernel, d_ff=w_down.rows, n_sub=n_sub, staged=staged),
        grid=(t // tm,),
        in_specs=[tile, _gain_spec_d(row_pre, d), _gain_spec_d(row_post, d)] + w_specs,
        out_specs=tile,
        out_shape=jax.ShapeDtypeStruct((t, d), F32),
        scratch_shapes=w_scratch,
        compiler_params=_params(1),
        name="ffn",
    )(x2, gains, gains, *w_args)


def _ple_kernel(x_ref, p_ref, g_ref, wg_hbm, wp_hbm, o_ref, *scratch, n_sub, staged):
    wg_ref, wp_ref = _load_staged(staged, (wg_hbm, wp_hbm), scratch, pl.program_id(0) == 0)
    ts = x_ref.shape[0] // n_sub
    for sub in range(n_sub):
        rows = slice(sub * ts, (sub + 1) * ts)
        x = x_ref[rows, :]
        xn = _rms(x, g_ref[...]).astype(BF16)
        gate = jax.nn.sigmoid(_dot(xn, wg_ref[...]))
        proj = _dot(p_ref[rows, :].astype(BF16), wp_ref[...])
        o_ref[rows, :] = x + gate * proj


def _ple(x2, p3, layer, gains, row, w_gate, w_proj, *, tm, n_sub):
    t, d = x2.shape
    pd = p3.shape[-1]
    staged = (w_gate, w_proj)
    w_specs, w_args, w_scratch = _staged_specs(staged)
    tile = pl.BlockSpec((tm, d), lambda i: (i, 0))
    return pl.pallas_call(
        functools.partial(_ple_kernel, n_sub=n_sub, staged=staged),
        grid=(t // tm,),
        in_specs=[tile, pl.BlockSpec((None, tm, pd), lambda i: (layer, i, 0)),
                  _gain_spec_d(row, d)] + w_specs,
        out_specs=tile,
        out_shape=jax.ShapeDtypeStruct((t, d), F32),
        scratch_shapes=w_scratch,
        compiler_params=_params(1),
        name="ple",
    )(x2, p3, gains, *w_args)


def _first_step_2d():
    return (pl.program_id(0) == 0) & (pl.program_id(1) == 0)


def _conv_kernel(x_ref, gpre_ref, gpost_ref, cw_ref, win_hbm, wout_hbm, o_ref, u_buf, *scratch,
                 n_sub, staged):
    win_ref, wout_ref = _load_staged(staged, (win_hbm, wout_hbm), scratch, _first_step_2d())
    tm, d = x_ref.shape
    ts = tm // n_sub

    @pl.when(pl.program_id(1) == 0)
    def _():
        u_buf[0:SUBLANES, :] = jnp.zeros((SUBLANES, d), F32)

    def in_proj(sub):
        h = _rms(x_ref[sub * ts:(sub + 1) * ts, :], gpre_ref[...]).astype(BF16)
        return tuple(_dot(h, win_ref[:, k * d:(k + 1) * d]) for k in range(3))

    cw = cw_ref[...]
    u = None
    pieces = in_proj(0)
    for sub in range(n_sub):
        r0 = sub * ts
        bgate, cgate, v = pieces
        u = cgate * v
        u_buf[SUBLANES + r0:SUBLANES + r0 + ts, :] = u
        conv = (cw[2:3, :] * u
                + cw[1:2, :] * u_buf[SUBLANES - 1 + r0:SUBLANES - 1 + r0 + ts, :]
                + cw[0:1, :] * u_buf[SUBLANES - 2 + r0:SUBLANES - 2 + r0 + ts, :])
        if sub + 1 < n_sub:
            pieces = in_proj(sub + 1)
        y = _dot((bgate * conv).astype(BF16), wout_ref[...])
        o_ref[r0:r0 + ts, :] = x_ref[r0:r0 + ts, :] + _rms(y, gpost_ref[...])
    u_buf[0:SUBLANES, :] = u[ts - SUBLANES:, :]


def _conv_mixer(x3, gains, row_pre, row_post, w_in, conv_w, w_out, *, tm, n_sub):
    b, s, d = x3.shape
    staged = (w_in, w_out)
    w_specs, w_args, w_scratch = _staged_specs(staged)
    tile = pl.BlockSpec((None, tm, d), lambda i, j: (i, j, 0))
    return pl.pallas_call(
        functools.partial(_conv_kernel, n_sub=n_sub, staged=staged),
        grid=(b, s // tm),
        in_specs=[tile, _gain_spec_d(row_pre, d), _gain_spec_d(row_post, d),
                  _resident(conv_w.shape)] + w_specs,
        out_specs=tile,
        out_shape=jax.ShapeDtypeStruct((b, s, d), F32),
        scratch_shapes=[pltpu.VMEM((tm + SUBLANES, d), F32)] + w_scratch,
        compiler_params=_params(2),
        name="conv_mixer",
    )(x3, gains, gains, conv_w, *w_args)


def _rel_bucket(dist):
    max_exact = REL_BUCKETS // 2
    dd = jnp.maximum(dist, 1).astype(F32)
    large = max_exact + (jnp.log(dd / max_exact) / math.log(REL_MAX_DIST / max_exact)
                         * (REL_BUCKETS - max_exact)).astype(jnp.int32)
    large = jnp.minimum(large, REL_BUCKETS - 1)
    return jnp.where(dist < max_exact, dist, large)


def _bias_kernel(rb_ref, bucket_ref, o_ref):
    bucket = bucket_ref[...]
    qi = lax.broadcasted_iota(jnp.int32, bucket.shape, 0)
    kj = lax.broadcasted_iota(jnp.int32, bucket.shape, 1)
    causal = kj <= qi
    hits = [bucket == bkt for bkt in range(REL_BUCKETS)]
    neg = jnp.full(bucket.shape, -jnp.inf, F32)
    for h in range(ATT_HEADS):
        acc = jnp.zeros(bucket.shape, F32)
        for bkt in range(REL_BUCKETS):
            acc = jnp.where(hits[bkt], rb_ref[bkt, h], acc)
        o_ref[0, h] = jnp.where(causal, acc, neg)
        o_ref[1, h] = acc


def _bias_tables(rel_bias):
    assert WINDOW == BLOCK
    qi = jnp.arange(BLOCK)[:, None]
    kj = jnp.arange(BLOCK)[None, :]
    dist = jnp.where(kj <= qi, qi - kj, qi + BLOCK - kj)
    bucket = _rel_bucket(dist).astype(jnp.int32)
    out = pl.pallas_call(
        _bias_kernel,
        in_specs=[pl.BlockSpec(memory_space=pltpu.SMEM),
                  pl.BlockSpec(memory_space=pltpu.VMEM)],
        out_specs=pl.BlockSpec(memory_space=pltpu.VMEM),
        out_shape=jax.ShapeDtypeStruct((2, ATT_HEADS, BLOCK, BLOCK), F32),
        name="rel_bias_table",
    )(rel_bias, bucket)
    return out.reshape(2, ATT_KV_HEADS, ATT_GROUP * BLOCK, BLOCK)


def _swa_kernel(x_ref, xprev_ref, gpre_ref, gpost_ref, bias_ref, sink_ref, wqkv_hbm, wo_hbm, o_ref,
                kk_buf, vv_buf, att_buf, *scratch, staged):
    wqkv_ref, wo_ref = _load_staged(staged, (wqkv_hbm, wo_hbm), scratch, _first_step_2d())
    tm, d = x_ref.shape
    n_blk = tm // BLOCK
    dq = ATT_HEADS * ATT_HEAD_DIM
    dkv = ATT_KV_HEADS * ATT_HEAD_DIM
    first_tile = pl.program_id(1) == 0

    x = x_ref[...]
    h = _rms(x, gpre_ref[...]).astype(BF16)
    qkv = _dot(h, wqkv_ref[...])
    q = (qkv[:, :dq] * (ATT_HEAD_DIM ** -0.5)).astype(BF16)
    h_prev = _rms(xprev_ref[...], gpre_ref[...]).astype(BF16)
    kv_prev = _dot(h_prev, wqkv_ref[:, dq:])
    k = jnp.concatenate([kv_prev[:, :dkv], qkv[:, dq:dq + dkv]], axis=0)
    v = jnp.concatenate([kv_prev[:, dkv:], qkv[:, dq + dkv:]], axis=0)

    low = lax.broadcasted_iota(jnp.int32, k.shape, 1) < ATT_HEAD_DIM
    k_sw = pltpu.roll(k, ATT_HEAD_DIM, 1)
    v_sw = pltpu.roll(v, ATT_HEAD_DIM, 1)
    kk_buf[0] = jnp.where(low, k, k_sw).astype(BF16)
    kk_buf[1] = jnp.where(low, k_sw, k).astype(BF16)
    vv_buf[0, :, 0:LANES] = jnp.where(low, v, v_sw).astype(BF16)
    vv_buf[1, :, 0:LANES] = jnp.where(low, v_sw, v).astype(BF16)
    vv_buf[:, :, LANES:] = jnp.ones((ATT_KV_HEADS, tm + BLOCK, LANES), BF16)

    low_b = lax.broadcasted_iota(jnp.int32, (BLOCK, LANES), 1) < ATT_HEAD_DIM
    zero = jnp.zeros((BLOCK, LANES), BF16)
    pairs = ATT_GROUP // 2
    stacked = (2 * BLOCK, BLOCK)
    q_pos = lax.broadcasted_iota(jnp.int32, stacked, 0) & (BLOCK - 1)
    causal = lax.broadcasted_iota(jnp.int32, stacked, 1) <= q_pos
    zero_p = jnp.zeros(stacked, BF16)
    for n in range(n_blk):
        rows = slice(n * BLOCK, (n + 1) * BLOCK)
        variant = jnp.where(first_tile, 0, 1) if n == 0 else 1
        for hk in range(ATT_KV_HEADS):
            kb = kk_buf[hk, n * BLOCK:(n + 2) * BLOCK, :]
            vb = vv_buf[hk, n * BLOCK:(n + 2) * BLOCK, :]
            for j in range(pairs):
                col = (hk * pairs + j) * LANES
                pair_rows = slice(2 * j * BLOCK, (2 * j + 2) * BLOCK)
                q2 = q[rows, col:col + LANES]
                qs = jnp.concatenate([jnp.where(low_b, q2, zero), jnp.where(low_b, zero, q2)], axis=0)
                s2 = _dot_nt(qs, kb)
                logits = (jnp.where(causal, s2[:, BLOCK:], s2[:, :BLOCK])
                          + bias_ref[variant, hk, pair_rows, :])
                sink = sink_ref[hk, pair_rows, :]
                m = jnp.maximum(jnp.max(logits, axis=-1, keepdims=True), sink)
                e = jnp.exp(logits - m).astype(BF16)
                e2 = jnp.concatenate([jnp.where(causal, zero_p, e),
                                      jnp.where(causal, e, zero_p)], axis=1)
                r = _dot(e2, vb)
                out = r[:, :LANES] / (r[:, LANES:] + jnp.exp(sink - m))
                att_buf[rows, col:col + LANES] = jnp.where(low_b, out[:BLOCK], out[BLOCK:]).astype(BF16)

    y = _dot(att_buf[...], wo_ref[...])
    o_ref[...] = x + _rms(y, gpost_ref[...])


def _swa_mixer(x3, gains, row_pre, row_post, w_qkv, bias_tab, sinks, w_o, *, tm):
    b, s, d = x3.shape
    blocks_per_tile = tm // BLOCK
    tile = pl.BlockSpec((None, tm, d), lambda i, j: (i, j, 0))
    prev_block = pl.BlockSpec((None, BLOCK, d),
                              lambda i, j: (i, jnp.maximum(j * blocks_per_tile - 1, 0), 0))
    sink_col = jnp.repeat(sinks.astype(F32).reshape(ATT_KV_HEADS, ATT_GROUP), BLOCK, axis=1)
    sink_col = jnp.broadcast_to(sink_col[..., None], sink_col.shape + (LANES,))
    staged = (w_qkv, w_o)
    w_specs, w_args, w_scratch = _staged_specs(staged)
    return pl.pallas_call(
        functools.partial(_swa_kernel, staged=staged),
        grid=(b, s // tm),
        in_specs=[tile, prev_block, _gain_spec_d(row_pre, d), _gain_spec_d(row_post, d),
                  _resident(bias_tab.shape), _resident(sink_col.shape)] + w_specs,
        out_specs=tile,
        out_shape=jax.ShapeDtypeStruct((b, s, d), F32),
        scratch_shapes=[pltpu.VMEM((ATT_KV_HEADS, tm + BLOCK, LANES), BF16),
                        pltpu.VMEM((ATT_KV_HEADS, tm + BLOCK, 2 * LANES), BF16),
                        pltpu.VMEM((tm, ATT_HEADS * ATT_HEAD_DIM), BF16)] + w_scratch,
        compiler_params=_params(2),
        name="swa_mixer",
    )(x3, x3, gains, gains, bias_tab, sink_col, *w_args)


def _ret_tables(s, dk):
    f32 = F32
    half = dk // 2
    inv = ROPE_BASE ** (-jnp.arange(half, dtype=f32) / half)
    ang = jnp.arange(s)[:, None].astype(f32) * inv[None, :]
    c = RET_CHUNK
    log_g = jnp.log(1.0 - 2.0 ** (-5.0 - jnp.arange(RET_HEADS, dtype=f32)))
    idx = jnp.arange(c, dtype=f32)
    diff = idx[:, None] - idx[None, :]
    decay_mask = jnp.where(diff >= 0, jnp.exp(log_g[:, None, None] * jnp.maximum(diff, 0.0)), 0.0)
    q_decay = jnp.exp(log_g[:, None] * (idx + 1.0))[..., None]
    k_decay = jnp.exp(log_g[:, None] * (c - 1.0 - idx))[..., None]
    chunk_decay = jnp.exp(log_g * c)
    return jnp.cos(ang), jnp.sin(ang), decay_mask, q_decay, k_decay, chunk_decay


def _ret_kernel(x_ref, gpre_ref, gpost_ref, cos_ref, sin_ref, dmask_ref, qdec_ref, kdec_ref,
                cdec_ref, w_hbm, wo_hbm, o_ref, state_ref, gated_buf, *scratch,
                dk, dv, n_sub, staged):
    w_ref, wo_ref = _load_staged(staged, (w_hbm, wo_hbm), scratch, _first_step_2d())
    tm, d = x_ref.shape
    ts = tm // n_sub
    n_chunk = ts // RET_CHUNK
    half = dk // 2
    hq = RET_HEADS * dk

    @pl.when(pl.program_id(1) == 0)
    def _():
        state_ref[...] = jnp.zeros(state_ref.shape, F32)

    def normed(sub):
        return _rms(x_ref[sub * ts:(sub + 1) * ts, :], gpre_ref[...]).astype(BF16)

    def head_proj(h, hd):
        return tuple(_dot(h, w_ref[:, c0:c0 + width]) for c0, width in (
            (hd * dk, dk), (hq + hd * dk, dk), (2 * hq + hd * dv, dv),
            (2 * hq + RET_HEADS * dv + hd * dv, dv)))

    pieces = [head_proj(normed(0), hd) for hd in range(RET_HEADS)]
    for sub in range(n_sub):
        r0 = sub * ts
        x = x_ref[r0:r0 + ts, :]
        cos = cos_ref[r0:r0 + ts, :]
        sin = sin_ref[r0:r0 + ts, :]
        h_next = normed(sub + 1) if sub + 1 < n_sub else None
        next_pieces = []

        def rot(t):
            t1, t2 = t[:, :half], t[:, half:]
            return jnp.concatenate([t1 * cos - t2 * sin, t1 * sin + t2 * cos], axis=-1)

        for hd in range(RET_HEADS):
            q_raw, k_raw, v, g = pieces[hd]
            q = rot(q_raw)
            k = rot(k_raw) * (dk ** -0.5)
            if h_next is not None:
                next_pieces.append(head_proj(h_next, hd))
            dmask = dmask_ref[hd]
            qdec = qdec_ref[hd]
            kdec = kdec_ref[hd]
            cdec = cdec_ref[hd]
            state = state_ref[hd]
            for c in range(n_chunk):
                rows = slice(c * RET_CHUNK, (c + 1) * RET_CHUNK)
                qc = q[rows].astype(BF16)
                kc = k[rows]
                vc = v[rows].astype(BF16)
                inner = _dot_nt(qc, kc.astype(BF16)) * dmask
                o = _dot(inner.astype(BF16), vc) + _dot(qc, state.astype(BF16)) * qdec
                state = state * cdec + _dot_tn((kc * kdec).astype(BF16), vc)
                mu = jnp.mean(o, axis=-1, keepdims=True)
                oc = o - mu
                var = jnp.mean(oc * oc, axis=-1, keepdims=True)
                on = oc * lax.rsqrt(var + NORM_EPS)
                gated_buf[r0 + c * RET_CHUNK:r0 + (c + 1) * RET_CHUNK, hd * dv:(hd + 1) * dv] = (
                    jax.nn.silu(g[rows]) * on).astype(BF16)
            state_ref[hd] = state

        pieces = next_pieces
        y = _dot(gated_buf[r0:r0 + ts, :], wo_ref[...])
        o_ref[r0:r0 + ts, :] = x + _rms(y, gpost_ref[...])


def _ret_mixer(x3, gains, row_pre, row_post, w_qkvg, w_o, *, tm, n_sub):
    b, s, d = x3.shape
    dk = d // RET_HEADS
    dv = 2 * dk
    cos, sin, dmask, qdec, kdec, cdec = _ret_tables(s, dk)
    cdec = cdec.reshape(RET_HEADS, 1, 1)
    tile = pl.BlockSpec((None, tm, d), lambda i, j: (i, j, 0))
    rope = pl.BlockSpec((tm, dk // 2), lambda i, j: (j, 0))
    staged = (w_qkvg, w_o)
    w_specs, w_args, w_scratch = _staged_specs(staged)
    return pl.pallas_call(
        functools.partial(_ret_kernel, dk=dk, dv=dv, n_sub=n_sub, staged=staged),
        grid=(b, s // tm),
        in_specs=[tile, _gain_spec_d(row_pre, d), _gain_spec_d(row_post, d),
                  rope, rope, _resident(dmask.shape),
                  _resident(qdec.shape), _resident(kdec.shape), _resident(cdec.shape)] + w_specs,
        out_specs=tile,
        out_shape=jax.ShapeDtypeStruct((b, s, d), F32),
        scratch_shapes=[pltpu.VMEM((RET_HEADS, dk, dv), F32),
                        pltpu.VMEM((tm, RET_HEADS * dv), BF16)] + w_scratch,
        compiler_params=_params(2),
        name="ret_mixer",
    )(x3, gains, gains, cos, sin, dmask, qdec, kdec, cdec, *w_args)


def _tile(n, want):
    t = min(n, want)
    assert n % t == 0, (n, t)
    return t


def kernel(x, p, norm_g, ffn_w_gu, ffn_w_down, ple_w_proj, ple_w_gate, rel_bias, conv_w_in, conv_w,
           conv_w_out, swa_w_qkv, swa_sinks, swa_w_o, ret_w_qkvg, ret_w_o):
    depth = norm_g.shape[0]
    b, s, d = x.shape
    t = b * s
    gains = norm_g.astype(F32).reshape(depth * N_NORMS, 1, d)
    p3 = p.reshape(depth, t, p.shape[-1])
    bias_tab = _bias_tables(rel_bias.astype(F32)) if depth > 1 else None

    tm_ffn = _tile(t, 1024)
    tm_ple = _tile(t, 1024)
    tm_conv = _tile(s, 1024)
    tm_swa = _tile(s, 512)
    tm_ret = _tile(s, 512)

    for i in range(depth):
        kind, j = i % N_MIXERS, i // N_MIXERS
        r = i * N_NORMS
        x2 = _ffn(x.reshape(t, d), gains, r + 0, r + 1,
                  _Staged(ffn_w_gu, (i, 0)), _Staged(ffn_w_down, (i, 0)),
                  tm=tm_ffn, n_sub=4)
        x = x2.reshape(b, s, d)
        if kind == 0:
            x = _conv_mixer(x, gains, r + 2, r + 3, _Staged(conv_w_in, (j,)), conv_w[j].astype(F32),
                            _Staged(conv_w_out, (j,)), tm=tm_conv, n_sub=4)
        elif kind == 1:
            x = _swa_mixer(x, gains, r + 2, r + 3, _Staged(swa_w_qkv, (j,)), bias_tab, swa_sinks[j],
                           _Staged(swa_w_o, (j,)), tm=tm_swa)
        else:
            x = _ret_mixer(x, gains, r + 2, r + 3, _Staged(ret_w_qkvg, (j,)), _Staged(ret_w_o, (j,)),
                           tm=tm_ret, n_sub=2)
        x2 = _ffn(x.reshape(t, d), gains, r + 4, r + 5,
                  _Staged(ffn_w_gu, (i, 1)), _Staged(ffn_w_down, (i, 1)),
                  tm=tm_ffn, n_sub=4)
        x2 = _ple(x2, p3, i, gains, r + 6, _Staged(ple_w_gate, (i,)), _Staged(ple_w_proj, (i,)),
                  tm=tm_ple, n_sub=4)
        x = x2.reshape(b, s, d)
    return x
```

```python
import functools
import math

import jax
import jax.numpy as jnp
from jax import lax
from jax.experimental import pallas as pl
from jax.experimental.pallas import tpu as pltpu

N_MIXERS = 3
N_NORMS = 7
NORM_EPS = 1e-6
CONV_WIDTH = 3
ATT_HEADS = 16
ATT_KV_HEADS = 2
ATT_HEAD_DIM = 64
ATT_GROUP = ATT_HEADS // ATT_KV_HEADS
WINDOW = 128
BLOCK = 128
REL_BUCKETS = 32
REL_MAX_DIST = 128
RET_HEADS = 4
RET_CHUNK = 256
ROPE_BASE = 10000.0

LANES = 128
SUBLANES = 8
VMEM_LIMIT_BYTES = 56 * 1024 * 1024
STAGE_SLOT_ELEMS = 384 * 1024
STAGE_SLOTS = 4

F32 = jnp.float32
BF16 = jnp.bfloat16


def _rms(x, g):
    ms = jnp.mean(x * x, axis=-1, keepdims=True)
    return x * lax.rsqrt(ms + NORM_EPS) * g


def _dot(a, b):
    return jnp.dot(a, b, preferred_element_type=F32)


def _dot_nt(a, b):
    return lax.dot_general(a, b, (((1,), (1,)), ((), ())), preferred_element_type=F32)


def _dot_tn(a, b):
    return lax.dot_general(a, b, (((0,), (0,)), ((), ())), preferred_element_type=F32)


def _resident(shape):
    nd = len(shape)
    return pl.BlockSpec(shape, lambda *_: (0,) * nd, pipeline_mode=pl.Buffered(1))


def _stage_chunk_rows(n_rows, n_cols):
    rows = 1 << ((STAGE_SLOT_ELEMS // n_cols).bit_length() - 1)
    rows = min(rows, n_rows)
    while n_rows % rows:
        rows //= 2
    return rows


class _Staged:
    def __init__(self, array, prefix=()):
        self.array, self.prefix = array, tuple(prefix)
        self.rows, self.cols = array.shape[len(self.prefix):]
        self.chunk = _stage_chunk_rows(self.rows, self.cols)
        self.slots = min(STAGE_SLOTS, self.rows // self.chunk)

    def scratch(self):
        return [pltpu.VMEM((self.rows, self.cols), BF16),
                pltpu.VMEM((self.slots, self.chunk, self.cols), self.array.dtype),
                pltpu.SemaphoreType.DMA((self.slots,))]

    def load(self, hbm_ref, vmem_ref, stage_ref, sem_ref):
        src = hbm_ref.at[self.prefix] if self.prefix else hbm_ref
        n_chunks = self.rows // self.chunk
        n_slots = self.slots
        ahead = max(n_slots - 1, 1)

        def copy(c):
            slot = c % n_slots
            return pltpu.make_async_copy(src.at[pl.ds(c * self.chunk, self.chunk)],
                                         stage_ref.at[slot], sem_ref.at[slot])

        for c in range(min(ahead, n_chunks)):
            copy(c).start()
        for c in range(n_chunks):
            if c + ahead < n_chunks:
                copy(c + ahead).start()
            copy(c).wait()
            vmem_ref[c * self.chunk:(c + 1) * self.chunk, :] = (
                stage_ref[c % n_slots].astype(BF16))


def _load_staged(staged, hbm_refs, scratch, first_step):
    groups = [scratch[3 * k:3 * k + 3] for k in range(len(staged))]

    @pl.when(first_step)
    def _():
        for w, hbm_ref, (vmem_ref, stage_ref, sem_ref) in zip(staged, hbm_refs, groups):
            w.load(hbm_ref, vmem_ref, stage_ref, sem_ref)

    return [g[0] for g in groups]


def _staged_specs(staged):
    in_specs = [pl.BlockSpec(memory_space=pl.ANY) for _ in staged]
    scratch = [s for w in staged for s in w.scratch()]
    return in_specs, [w.array for w in staged], scratch


def _gain_spec_d(row, d):
    return pl.BlockSpec((None, 1, d), lambda *_: (row, 0, 0))


def _params(n_grid):
    return pltpu.CompilerParams(
        dimension_semantics=("arbitrary",) * n_grid,
        vmem_limit_bytes=VMEM_LIMIT_BYTES,
    )


def _ffn_kernel(x_ref, gpre_ref, gpost_ref, wgu_hbm, wd_hbm, o_ref, *scratch, d_ff, n_sub, staged):
    wgu_ref, wd_ref = _load_staged(staged, (wgu_hbm, wd_hbm), scratch, pl.program_id(0) == 0)
    ts = x_ref.shape[0] // n_sub
    for sub in range(n_sub):
        rows = slice(sub * ts, (sub + 1) * ts)
        x = x_ref[rows, :]
        xn = _rms(x, gpre_ref[...]).astype(BF16)
        a = _dot(xn, wgu_ref[:, :d_ff])
        b = _dot(xn, wgu_ref[:, d_ff:])
        y = _dot((jax.nn.silu(a) * b).astype(BF16), wd_ref[...])
        o_ref[rows, :] = x + 0.5 * _rms(y, gpost_ref[...])


def _ffn(x2, gains, row_pre, row_post, w_gu, w_down, *, tm, n_sub):
    t, d = x2.shape
    staged = (w_gu, w_down)
    w_specs, w_args, w_scratch = _staged_specs(staged)
    tile = pl.BlockSpec((tm, d), lambda i: (i, 0))
    return pl.pallas_call(
        functools.partial(_ffn_kernel, d_ff=w_down.rows, n_sub=n_sub, staged=staged),
        grid=(t // tm,),
        in_specs=[tile, _gain_spec_d(row_pre, d), _gain_spec_d(row_post, d)] + w_specs,
        out_specs=tile,
        out_shape=jax.ShapeDtypeStruct((t, d), F32),
        scratch_shapes=w_scratch,
        compiler_params=_params(1),
        name="ffn",
    )(x2, gains, gains, *w_args)


def _ple_kernel(x_ref, p_ref, g_ref, wg_hbm, wp_hbm, o_ref, *scratch, n_sub, staged):
    wg_ref, wp_ref = _load_staged(staged, (wg_hbm, wp_hbm), scratch, pl.program_id(0) == 0)
    ts = x_ref.shape[0] // n_sub
    for sub in range(n_sub):
        rows = slice(sub * ts, (sub + 1) * ts)
        x = x_ref[rows, :]
        xn = _rms(x, g_ref[...]).astype(BF16)
        gate = jax.nn.sigmoid(_dot(xn, wg_ref[...]))
        proj = _dot(p_ref[rows, :].astype(BF16), wp_ref[...])
        o_ref[rows, :] = x + gate * proj


def _ple(x2, p3, layer, gains, row, w_gate, w_proj, *, tm, n_sub):
    t, d = x2.shape
    pd = p3.shape[-1]
    staged = (w_gate, w_proj)
    w_specs, w_args, w_scratch = _staged_specs(staged)
    tile = pl.BlockSpec((tm, d), lambda i: (i, 0))
    return pl.pallas_call(
        functools.partial(_ple_kernel, n_sub=n_sub, staged=staged),
        grid=(t // tm,),
        in_specs=[tile, pl.BlockSpec((None, tm, pd), lambda i: (layer, i, 0)),
                  _gain_spec_d(row, d)] + w_specs,
        out_specs=tile,
        out_shape=jax.ShapeDtypeStruct((t, d), F32),
        scratch_shapes=w_scratch,
        compiler_params=_params(1),
        name="ple",
    )(x2, p3, gains, *w_args)


def _first_step_2d():
    return (pl.program_id(0) == 0) & (pl.program_id(1) == 0)


def _conv_kernel(x_ref, gpre_ref, gpost_ref, cw_ref, win_hbm, wout_hbm, o_ref, u_buf, *scratch,
                 n_sub, staged):
    win_ref, wout_ref = _load_staged(staged, (win_hbm, wout_hbm), scratch, _first_step_2d())
    tm, d = x_ref.shape
    ts = tm // n_sub

    @pl.when(pl.program_id(1) == 0)
    def _():
        u_buf[0:SUBLANES, :] = jnp.zeros((SUBLANES, d), F32)

    def in_proj(sub):
        h = _rms(x_ref[sub * ts:(sub + 1) * ts, :], gpre_ref[...]).astype(BF16)
        return tuple(_dot(h, win_ref[:, k * d:(k + 1) * d]) for k in range(3))

    cw = cw_ref[...]
    u = None
    pieces = in_proj(0)
    for sub in range(n_sub):
        r0 = sub * ts
        bgate, cgate, v = pieces
        u = cgate * v
        u_buf[SUBLANES + r0:SUBLANES + r0 + ts, :] = u
        conv = (cw[2:3, :] * u
                + cw[1:2, :] * u_buf[SUBLANES - 1 + r0:SUBLANES - 1 + r0 + ts, :]
                + cw[0:1, :] * u_buf[SUBLANES - 2 + r0:SUBLANES - 2 + r0 + ts, :])
        if sub + 1 < n_sub:
            pieces = in_proj(sub + 1)
        y = _dot((bgate * conv).astype(BF16), wout_ref[...])
        o_ref[r0:r0 + ts, :] = x_ref[r0:r0 + ts, :] + _rms(y, gpost_ref[...])
    u_buf[0:SUBLANES, :] = u[ts - SUBLANES:, :]


def _conv_mixer(x3, gains, row_pre, row_post, w_in, conv_w, w_out, *, tm, n_sub):
    b, s, d = x3.shape
    staged = (w_in, w_out)
    w_specs, w_args, w_scratch = _staged_specs(staged)
    tile = pl.BlockSpec((None, tm, d), lambda i, j: (i, j, 0))
    return pl.pallas_call(
        functools.partial(_conv_kernel, n_sub=n_sub, staged=staged),
        grid=(b, s // tm),
        in_specs=[tile, _gain_spec_d(row_pre, d), _gain_spec_d(row_post, d),
                  _resident(conv_w.shape)] + w_specs,
        out_specs=tile,
        out_shape=jax.ShapeDtypeStruct((b, s, d), F32),
        scratch_shapes=[pltpu.VMEM((tm + SUBLANES, d), F32)] + w_scratch,
        compiler_params=_params(2),
        name="conv_mixer",
    )(x3, gains, gains, conv_w, *w_args)


def _rel_bucket(dist):
    max_exact = REL_BUCKETS // 2
    dd = jnp.maximum(dist, 1).astype(F32)
    large = max_exact + (jnp.log(dd / max_exact) / math.log(REL_MAX_DIST / max_exact)
                         * (REL_BUCKETS - max_exact)).astype(jnp.int32)
    large = jnp.minimum(large, REL_BUCKETS - 1)
    return jnp.where(dist < max_exact, dist, large)


def _bias_kernel(rb_ref, bucket_ref, o_ref):
    bucket = bucket_ref[...]
    qi = lax.broadcasted_iota(jnp.int32, bucket.shape, 0)
    kj = lax.broadcasted_iota(jnp.int32, bucket.shape, 1)
    causal = kj <= qi
    hits = [bucket == bkt for bkt in range(REL_BUCKETS)]
    neg = jnp.full(bucket.shape, -jnp.inf, F32)
    for h in range(ATT_HEADS):
        acc = jnp.zeros(bucket.shape, F32)
        for bkt in range(REL_BUCKETS):
            acc = jnp.where(hits[bkt], rb_ref[bkt, h], acc)
        o_ref[0, h] = jnp.where(causal, acc, neg)
        o_ref[1, h] = acc


def _bias_tables(rel_bias):
    assert WINDOW == BLOCK
    qi = jnp.arange(BLOCK)[:, None]
    kj = jnp.arange(BLOCK)[None, :]
    dist = jnp.where(kj <= qi, qi - kj, qi + BLOCK - kj)
    bucket = _rel_bucket(dist).astype(jnp.int32)
    out = pl.pallas_call(
        _bias_kernel,
        in_specs=[pl.BlockSpec(memory_space=pltpu.SMEM),
                  pl.BlockSpec(memory_space=pltpu.VMEM)],
        out_specs=pl.BlockSpec(memory_space=pltpu.VMEM),
        out_shape=jax.ShapeDtypeStruct((2, ATT_HEADS, BLOCK, BLOCK), F32),
        name="rel_bias_table",
    )(rel_bias, bucket)
    return out.reshape(2, ATT_KV_HEADS, ATT_GROUP * BLOCK, BLOCK)


def _swa_kernel(x_ref, xprev_ref, gpre_ref, gpost_ref, bias_ref, sink_ref, wqkv_hbm, wo_hbm, o_ref,
                kk_buf, vv_buf, att_buf, *scratch, staged):
    wqkv_ref, wo_ref = _load_staged(staged, (wqkv_hbm, wo_hbm), scratch, _first_step_2d())
    tm, d = x_ref.shape
    n_blk = tm // BLOCK
    dq = ATT_HEADS * ATT_HEAD_DIM
    dkv = ATT_KV_HEADS * ATT_HEAD_DIM
    first_tile = pl.program_id(1) == 0

    x = x_ref[...]
    h = _rms(x, gpre_ref[...]).astype(BF16)
    qkv = _dot(h, wqkv_ref[...])
    q = (qkv[:, :dq] * (ATT_HEAD_DIM ** -0.5)).astype(BF16)
    h_prev = _rms(xprev_ref[...], gpre_ref[...]).astype(BF16)
    kv_prev = _dot(h_prev, wqkv_ref[:, dq:])
    k = jnp.concatenate([kv_prev[:, :dkv], qkv[:, dq:dq + dkv]], axis=0)
    v = jnp.concatenate([kv_prev[:, dkv:], qkv[:, dq + dkv:]], axis=0)

    low = lax.broadcasted_iota(jnp.int32, k.shape, 1) < ATT_HEAD_DIM
    k_sw = pltpu.roll(k, ATT_HEAD_DIM, 1)
    v_sw = pltpu.roll(v, ATT_HEAD_DIM, 1)
    kk_buf[0] = jnp.where(low, k, k_sw).astype(BF16)
    kk_buf[1] = jnp.where(low, k_sw, k).astype(BF16)
    vv_buf[0, :, 0:LANES] = jnp.where(low, v, v_sw).astype(BF16)
    vv_buf[1, :, 0:LANES] = jnp.where(low, v_sw, v).astype(BF16)
    vv_buf[:, :, LANES:] = jnp.ones((ATT_KV_HEADS, tm + BLOCK, LANES), BF16)

    low_b = lax.broadcasted_iota(jnp.int32, (BLOCK, LANES), 1) < ATT_HEAD_DIM
    zero = jnp.zeros((BLOCK, LANES), BF16)
    pairs = ATT_GROUP // 2
    stacked = (2 * BLOCK, BLOCK)
    q_pos = lax.broadcasted_iota(jnp.int32, stacked, 0) & (BLOCK - 1)
    causal = lax.broadcasted_iota(jnp.int32, stacked, 1) <= q_pos
    zero_p = jnp.zeros(stacked, BF16)
    for n in range(n_blk):
        rows = slice(n * BLOCK, (n + 1) * BLOCK)
        variant = jnp.where(first_tile, 0, 1) if n == 0 else 1
        for hk in range(ATT_KV_HEADS):
            kb = kk_buf[hk, n * BLOCK:(n + 2) * BLOCK, :]
            vb = vv_buf[hk, n * BLOCK:(n + 2) * BLOCK, :]
            for j in range(pairs):
                col = (hk * pairs + j) * LANES
                pair_rows = slice(2 * j * BLOCK, (2 * j + 2) * BLOCK)
                q2 = q[rows, col:col + LANES]
                qs = jnp.concatenate([jnp.where(low_b, q2, zero), jnp.where(low_b, zero, q2)], axis=0)
                s2 = _dot_nt(qs, kb)
                logits = (jnp.where(causal, s2[:, BLOCK:], s2[:, :BLOCK])
                          + bias_ref[variant, hk, pair_rows, :])
                sink = sink_ref[hk, pair_rows, :]
                m = jnp.maximum(jnp.max(logits, axis=-1, keepdims=True), sink)
                e = jnp.exp(logits - m).astype(BF16)
                e2 = jnp.concatenate([jnp.where(causal, zero_p, e),
                                      jnp.where(causal, e, zero_p)], axis=1)
                r = _dot(e2, vb)
                out = r[:, :LANES] / (r[:, LANES:] + jnp.exp(sink - m))
                att_buf[rows, col:col + LANES] = jnp.where(low_b, out[:BLOCK], out[BLOCK:]).astype(BF16)

    y = _dot(att_buf[...], wo_ref[...])
    o_ref[...] = x + _rms(y, gpost_ref[...])


def _swa_mixer(x3, gains, row_pre, row_post, w_qkv, bias_tab, sinks, w_o, *, tm):
    b, s, d = x3.shape
    blocks_per_tile = tm // BLOCK
    tile = pl.BlockSpec((None, tm, d), lambda i, j: (i, j, 0))
    prev_block = pl.BlockSpec((None, BLOCK, d),
                              lambda i, j: (i, jnp.maximum(j * blocks_per_tile - 1, 0), 0))
    sink_col = jnp.repeat(sinks.astype(F32).reshape(ATT_KV_HEADS, ATT_GROUP), BLOCK, axis=1)
    sink_col = jnp.broadcast_to(sink_col[..., None], sink_col.shape + (LANES,))
    staged = (w_qkv, w_o)
    w_specs, w_args, w_scratch = _staged_specs(staged)
    return pl.pallas_call(
        functools.partial(_swa_kernel, staged=staged),
        grid=(b, s // tm),
        in_specs=[tile, prev_block, _gain_spec_d(row_pre, d), _gain_spec_d(row_post, d),
                  _resident(bias_tab.shape), _resident(sink_col.shape)] + w_specs,
        out_specs=tile,
        out_shape=jax.ShapeDtypeStruct((b, s, d), F32),
        scratch_shapes=[pltpu.VMEM((ATT_KV_HEADS, tm + BLOCK, LANES), BF16),
                        pltpu.VMEM((ATT_KV_HEADS, tm + BLOCK, 2 * LANES), BF16),
                        pltpu.VMEM((tm, ATT_HEADS * ATT_HEAD_DIM), BF16)] + w_scratch,
        compiler_params=_params(2),
        name="swa_mixer",
    )(x3, x3, gains, gains, bias_tab, sink_col, *w_args)


def _ret_tables(s, dk):
    f32 = F32
    half = dk // 2
    inv = ROPE_BASE ** (-jnp.arange(half, dtype=f32) / half)
    ang = jnp.arange(s)[:, None].astype(f32) * inv[None, :]
    c = RET_CHUNK
    log_g = jnp.log(1.0 - 2.0 ** (-5.0 - jnp.arange(RET_HEADS, dtype=f32)))
    idx = jnp.arange(c, dtype=f32)
    diff = idx[:, None] - idx[None, :]
    decay_mask = jnp.where(diff >= 0, jnp.exp(log_g[:, None, None] * jnp.maximum(diff, 0.0)), 0.0)
    q_decay = jnp.exp(log_g[:, None] * (idx + 1.0))[..., None]
    k_decay = jnp.exp(log_g[:, None] * (c - 1.0 - idx))[..., None]
    chunk_decay = jnp.exp(log_g * c)
    return jnp.cos(ang), jnp.sin(ang), decay_mask, q_decay, k_decay, chunk_decay


def _ret_kernel(x_ref, gpre_ref, gpost_ref, cos_ref, sin_ref, dmask_ref, qdec_ref, kdec_ref,
                cdec_ref, w_hbm, wo_hbm, o_ref, state_ref, gated_buf, *scratch,
                dk, dv, n_sub, staged):
    w_ref, wo_ref = _load_staged(staged, (w_hbm, wo_hbm), scratch, _first_step_2d())
    tm, d = x_ref.shape
    ts = tm // n_sub
    n_chunk = ts // RET_CHUNK
    half = dk // 2
    hq = RET_HEADS * dk

    @pl.when(pl.program_id(1) == 0)
    def _():
        state_ref[...] = jnp.zeros(state_ref.shape, F32)

    for sub in range(n_sub):
        r0 = sub * ts
        x = x_ref[r0:r0 + ts, :]
        h = _rms(x, gpre_ref[...]).astype(BF16)
        proj = _dot(h, w_ref[...])
        cos = cos_ref[r0:r0 + ts, :]
        sin = sin_ref[r0:r0 + ts, :]

        def rot(t):
            t1, t2 = t[:, :half], t[:, half:]
            return jnp.concatenate([t1 * cos - t2 * sin, t1 * sin + t2 * cos], axis=-1)

        for hd in range(RET_HEADS):
            q = rot(proj[:, hd * dk:(hd + 1) * dk])
            k = rot(proj[:, hq + hd * dk:hq + (hd + 1) * dk]) * (dk ** -0.5)
            v = proj[:, 2 * hq + hd * dv:2 * hq + (hd + 1) * dv]
            g = proj[:, 2 * hq + RET_HEADS * dv + hd * dv:2 * hq + RET_HEADS * dv + (hd + 1) * dv]
            dmask = dmask_ref[hd]
            qdec = qdec_ref[hd]
            kdec = kdec_ref[hd]
            cdec = cdec_ref[hd]
            state = state_ref[hd]
            for c in range(n_chunk):
                rows = slice(c * RET_CHUNK, (c + 1) * RET_CHUNK)
                qc = q[rows].astype(BF16)
                kc = k[rows]
                vc = v[rows].astype(BF16)
                inner = _dot_nt(qc, kc.astype(BF16)) * dmask
                o = _dot(inner.astype(BF16), vc) + _dot(qc, state.astype(BF16)) * qdec
                state = state * cdec + _dot_tn((kc * kdec).astype(BF16), vc)
                mu = jnp.mean(o, axis=-1, keepdims=True)
                oc = o - mu
                var = jnp.mean(oc * oc, axis=-1, keepdims=True)
                on = oc * lax.rsqrt(var + NORM_EPS)
                gated_buf[r0 + c * RET_CHUNK:r0 + (c + 1) * RET_CHUNK, hd * dv:(hd + 1) * dv] = (
                    jax.nn.silu(g[rows]) * on).astype(BF16)
            state_ref[hd] = state

        y = _dot(gated_buf[r0:r0 + ts, :], wo_ref[...])
        o_ref[r0:r0 + ts, :] = x + _rms(y, gpost_ref[...])


def _ret_mixer(x3, gains, row_pre, row_post, w_qkvg, w_o, *, tm, n_sub):
    b, s, d = x3.shape
    dk = d // RET_HEADS
    dv = 2 * dk
    cos, sin, dmask, qdec, kdec, cdec = _ret_tables(s, dk)
    cdec = cdec.reshape(RET_HEADS, 1, 1)
    tile = pl.BlockSpec((None, tm, d), lambda i, j: (i, j, 0))
    rope = pl.BlockSpec((tm, dk // 2), lambda i, j: (j, 0))
    staged = (w_qkvg, w_o)
    w_specs, w_args, w_scratch = _staged_specs(staged)
    return pl.pallas_call(
        functools.partial(_ret_kernel, dk=dk, dv=dv, n_sub=n_sub, staged=staged),
        grid=(b, s // tm),
        in_specs=[tile, _gain_spec_d(row_pre, d), _gain_spec_d(row_post, d),
                  rope, rope, _resident(dmask.shape),
                  _resident(qdec.shape), _resident(kdec.shape), _resident(cdec.shape)] + w_specs,
        out_specs=tile,
        out_shape=jax.ShapeDtypeStruct((b, s, d), F32),
        scratch_shapes=[pltpu.VMEM((RET_HEADS, dk, dv), F32),
                        pltpu.VMEM((tm, RET_HEADS * dv), BF16)] + w_scratch,
        compiler_params=_params(2),
        name="ret_mixer",
    )(x3, gains, gains, cos, sin, dmask, qdec, kdec, cdec, *w_args)


def _tile(n, want):
    t = min(n, want)
    assert n % t == 0, (n, t)
    return t


def kernel(x, p, norm_g, ffn_w_gu, ffn_w_down, ple_w_proj, ple_w_gate, rel_bias, conv_w_in, conv_w,
           conv_w_out, swa_w_qkv, swa_sinks, swa_w_o, ret_w_qkvg, ret_w_o):
    depth = norm_g.shape[0]
    b, s, d = x.shape
    t = b * s
    gains = norm_g.astype(F32).reshape(depth * N_NORMS, 1, d)
    p3 = p.reshape(depth, t, p.shape[-1])
    bias_tab = _bias_tables(rel_bias.astype(F32)) if depth > 1 else None

    tm_ffn = _tile(t, 1024)
    tm_ple = _tile(t, 2048)
    tm_conv = _tile(s, 1024)
    tm_swa = _tile(s, 512)
    tm_ret = _tile(s, 512)

    for i in range(depth):
        kind, j = i % N_MIXERS, i // N_MIXERS
        r = i * N_NORMS
        x2 = _ffn(x.reshape(t, d), gains, r + 0, r + 1,
                  _Staged(ffn_w_gu, (i, 0)), _Staged(ffn_w_down, (i, 0)),
                  tm=tm_ffn, n_sub=4)
        x = x2.reshape(b, s, d)
        if kind == 0:
            x = _conv_mixer(x, gains, r + 2, r + 3, _Staged(conv_w_in, (j,)), conv_w[j].astype(F32),
                            _Staged(conv_w_out, (j,)), tm=tm_conv, n_sub=4)
        elif kind == 1:
            x = _swa_mixer(x, gains, r + 2, r + 3, _Staged(swa_w_qkv, (j,)), bias_tab, swa_sinks[j],
                           _Staged(swa_w_o, (j,)), tm=tm_swa)
        else:
            x = _ret_mixer(x, gains, r + 2, r + 3, _Staged(ret_w_qkvg, (j,)), _Staged(ret_w_o, (j,)),
                           tm=tm_ret, n_sub=2)
        x2 = _ffn(x.reshape(t, d), gains, r + 4, r + 5,
                  _Staged(ffn_w_gu, (i, 1)), _Staged(ffn_w_down, (i, 1)),
                  tm=tm_ffn, n_sub=4)
        x2 = _ple(x2, p3, i, gains, r + 6, _Staged(ple_w_gate, (i,)), _Staged(ple_w_proj, (i,)),
                  tm=tm_ple, n_sub=8)
        x = x2.reshape(b, s, d)
    return x
```

```python
import functools
import math

import jax
import jax.numpy as jnp
from jax import lax
from jax.experimental import pallas as pl
from jax.experimental.pallas import tpu as pltpu

N_MIXERS = 3
N_NORMS = 7
NORM_EPS = 1e-6
CONV_WIDTH = 3
ATT_HEADS = 16
ATT_KV_HEADS = 2
ATT_HEAD_DIM = 64
ATT_GROUP = ATT_HEADS // ATT_KV_HEADS
WINDOW = 128
BLOCK = 128
REL_BUCKETS = 32
REL_MAX_DIST = 128
RET_HEADS = 4
RET_CHUNK = 256
ROPE_BASE = 10000.0

LANES = 128
SUBLANES = 8
VMEM_LIMIT_BYTES = 56 * 1024 * 1024
STAGE_SLOT_ELEMS = 384 * 1024
STAGE_SLOTS = 4

F32 = jnp.float32
BF16 = jnp.bfloat16


def _rms(x, g):
    ms = jnp.mean(x * x, axis=-1, keepdims=True)
    return x * lax.rsqrt(ms + NORM_EPS) * g


def _dot(a, b):
    return jnp.dot(a, b, preferred_element_type=F32)


def _dot_nt(a, b):
    return lax.dot_general(a, b, (((1,), (1,)), ((), ())), preferred_element_type=F32)


def _dot_tn(a, b):
    return lax.dot_general(a, b, (((0,), (0,)), ((), ())), preferred_element_type=F32)


def _resident(shape):
    nd = len(shape)
    return pl.BlockSpec(shape, lambda *_: (0,) * nd, pipeline_mode=pl.Buffered(1))


def _stage_chunk_rows(n_rows, n_cols):
    rows = 1 << ((STAGE_SLOT_ELEMS // n_cols).bit_length() - 1)
    rows = min(rows, n_rows)
    while n_rows % rows:
        rows //= 2
    return rows


class _Staged:
    def __init__(self, array, prefix=()):
        self.array, self.prefix = array, tuple(prefix)
        self.rows, self.cols = array.shape[len(self.prefix):]
        self.chunk = _stage_chunk_rows(self.rows, self.cols)
        self.slots = min(STAGE_SLOTS, self.rows // self.chunk)

    def scratch(self):
        return [pltpu.VMEM((self.rows, self.cols), BF16),
                pltpu.VMEM((self.slots, self.chunk, self.cols), self.array.dtype),
                pltpu.SemaphoreType.DMA((self.slots,))]

    def load(self, hbm_ref, vmem_ref, stage_ref, sem_ref):
        src = hbm_ref.at[self.prefix] if self.prefix else hbm_ref
        n_chunks = self.rows // self.chunk
        n_slots = self.slots
        ahead = max(n_slots - 1, 1)

        def copy(c):
            slot = c % n_slots
            return pltpu.make_async_copy(src.at[pl.ds(c * self.chunk, self.chunk)],
                                         stage_ref.at[slot], sem_ref.at[slot])

        for c in range(min(ahead, n_chunks)):
            copy(c).start()
        for c in range(n_chunks):
            if c + ahead < n_chunks:
                copy(c + ahead).start()
            copy(c).wait()
            vmem_ref[c * self.chunk:(c + 1) * self.chunk, :] = (
                stage_ref[c % n_slots].astype(BF16))


def _load_staged(staged, hbm_refs, scratch, first_step):
    groups = [scratch[3 * k:3 * k + 3] for k in range(len(staged))]

    @pl.when(first_step)
    def _():
        for w, hbm_ref, (vmem_ref, stage_ref, sem_ref) in zip(staged, hbm_refs, groups):
            w.load(hbm_ref, vmem_ref, stage_ref, sem_ref)

    return [g[0] for g in groups]


def _staged_specs(staged):
    in_specs = [pl.BlockSpec(memory_space=pl.ANY) for _ in staged]
    scratch = [s for w in staged for s in w.scratch()]
    return in_specs, [w.array for w in staged], scratch


def _gain_spec_d(row, d):
    return pl.BlockSpec((None, 1, d), lambda *_: (row, 0, 0))


def _params(n_grid):
    return pltpu.CompilerParams(
        dimension_semantics=("arbitrary",) * n_grid,
        vmem_limit_bytes=VMEM_LIMIT_BYTES,
    )


def _ffn_kernel(x_ref, gpre_ref, gpost_ref, wgu_hbm, wd_hbm, o_ref, *scratch, d_ff, n_sub, staged):
    wgu_ref, wd_ref = _load_staged(staged, (wgu_hbm, wd_hbm), scratch, pl.program_id(0) == 0)
    ts = x_ref.shape[0] // n_sub
    for sub in range(n_sub):
        rows = slice(sub * ts, (sub + 1) * ts)
        x = x_ref[rows, :]
        xn = _rms(x, gpre_ref[...]).astype(BF16)
        a = _dot(xn, wgu_ref[:, :d_ff])
        b = _dot(xn, wgu_ref[:, d_ff:])
        y = _dot((jax.nn.silu(a) * b).astype(BF16), wd_ref[...])
        o_ref[rows, :] = x + 0.5 * _rms(y, gpost_ref[...])


def _ffn(x2, gains, row_pre, row_post, w_gu, w_down, *, tm, n_sub):
    t, d = x2.shape
    staged = (w_gu, w_down)
    w_specs, w_args, w_scratch = _staged_specs(staged)
    tile = pl.BlockSpec((tm, d), lambda i: (i, 0))
    return pl.pallas_call(
        functools.partial(_ffn_kernel, d_ff=w_down.rows, n_sub=n_sub, staged=staged),
        grid=(t // tm,),
        in_specs=[tile, _gain_spec_d(row_pre, d), _gain_spec_d(row_post, d)] + w_specs,
        out_specs=tile,
        out_shape=jax.ShapeDtypeStruct((t, d), F32),
        scratch_shapes=w_scratch,
        compiler_params=_params(1),
        name="ffn",
    )(x2, gains, gains, *w_args)


def _ple_kernel(x_ref, p_ref, g_ref, wg_hbm, wp_hbm, o_ref, *scratch, n_sub, staged):
    wg_ref, wp_ref = _load_staged(staged, (wg_hbm, wp_hbm), scratch, pl.program_id(0) == 0)
    ts = x_ref.shape[0] // n_sub
    for sub in range(n_sub):
        rows = slice(sub * ts, (sub + 1) * ts)
        x = x_ref[rows, :]
        xn = _rms(x, g_ref[...]).astype(BF16)
        gate = jax.nn.sigmoid(_dot(xn, wg_ref[...]))
        proj = _dot(p_ref[rows, :].astype(BF16), wp_ref[...])
        o_ref[rows, :] = x + gate * proj


def _ple(x2, p3, layer, gains, row, w_gate, w_proj, *, tm, n_sub):
    t, d = x2.shape
    pd = p3.shape[-1]
    staged = (w_gate, w_proj)
    w_specs, w_args, w_scratch = _staged_specs(staged)
    tile = pl.BlockSpec((tm, d), lambda i: (i, 0))
    return pl.pallas_call(
        functools.partial(_ple_kernel, n_sub=n_sub, staged=staged),
        grid=(t // tm,),
        in_specs=[tile, pl.BlockSpec((None, tm, pd), lambda i: (layer, i, 0)),
                  _gain_spec_d(row, d)] + w_specs,
        out_specs=tile,
        out_shape=jax.ShapeDtypeStruct((t, d), F32),
        scratch_shapes=w_scratch,
        compiler_params=_params(1),
        name="ple",
    )(x2, p3, gains, *w_args)


def _first_step_2d():
    return (pl.program_id(0) == 0) & (pl.program_id(1) == 0)


def _conv_kernel(x_ref, gpre_ref, gpost_ref, cw_ref, win_hbm, wout_hbm, o_ref, u_buf, *scratch,
                 n_sub, staged):
    win_ref, wout_ref = _load_staged(staged, (win_hbm, wout_hbm), scratch, _first_step_2d())
    tm, d = x_ref.shape
    ts = tm // n_sub

    @pl.when(pl.program_id(1) == 0)
    def _():
        u_buf[0:SUBLANES, :] = jnp.zeros((SUBLANES, d), F32)

    def in_proj(sub):
        h = _rms(x_ref[sub * ts:(sub + 1) * ts, :], gpre_ref[...]).astype(BF16)
        return tuple(_dot(h, win_ref[:, k * d:(k + 1) * d]) for k in range(3))

    cw = cw_ref[...]
    u = None
    pieces = in_proj(0)
    for sub in range(n_sub):
        r0 = sub * ts
        bgate, cgate, v = pieces
        u = cgate * v
        u_buf[SUBLANES + r0:SUBLANES + r0 + ts, :] = u
        conv = (cw[2:3, :] * u
                + cw[1:2, :] * u_buf[SUBLANES - 1 + r0:SUBLANES - 1 + r0 + ts, :]
                + cw[0:1, :] * u_buf[SUBLANES - 2 + r0:SUBLANES - 2 + r0 + ts, :])
        if sub + 1 < n_sub:
            pieces = in_proj(sub + 1)
        y = _dot((bgate * conv).astype(BF16), wout_ref[...])
        o_ref[r0:r0 + ts, :] = x_ref[r0:r0 + ts, :] + _rms(y, gpost_ref[...])
    u_buf[0:SUBLANES, :] = u[ts - SUBLANES:, :]


def _conv_mixer(x3, gains, row_pre, row_post, w_in, conv_w, w_out, *, tm, n_sub):
    b, s, d = x3.shape
    staged = (w_in, w_out)
    w_specs, w_args, w_scratch = _staged_specs(staged)
    tile = pl.BlockSpec((None, tm, d), lambda i, j: (i, j, 0))
    return pl.pallas_call(
        functools.partial(_conv_kernel, n_sub=n_sub, staged=staged),
        grid=(b, s // tm),
        in_specs=[tile, _gain_spec_d(row_pre, d), _gain_spec_d(row_post, d),
                  _resident(conv_w.shape)] + w_specs,
        out_specs=tile,
        out_shape=jax.ShapeDtypeStruct((b, s, d), F32),
        scratch_shapes=[pltpu.VMEM((tm + SUBLANES, d), F32)] + w_scratch,
        compiler_params=_params(2),
        name="conv_mixer",
    )(x3, gains, gains, conv_w, *w_args)


def _rel_bucket(dist):
    max_exact = REL_BUCKETS // 2
    dd = jnp.maximum(dist, 1).astype(F32)
    large = max_exact + (jnp.log(dd / max_exact) / math.log(REL_MAX_DIST / max_exact)
                         * (REL_BUCKETS - max_exact)).astype(jnp.int32)
    large = jnp.minimum(large, REL_BUCKETS - 1)
    return jnp.where(dist < max_exact, dist, large)


def _bias_kernel(rb_ref, bucket_ref, o_ref):
    bucket = bucket_ref[...]
    qi = lax.broadcasted_iota(jnp.int32, bucket.shape, 0)
    kj = lax.broadcasted_iota(jnp.int32, bucket.shape, 1)
    causal = kj <= qi
    hits = [bucket == bkt for bkt in range(REL_BUCKETS)]
    neg = jnp.full(bucket.shape, -jnp.inf, F32)
    for h in range(ATT_HEADS):
        acc = jnp.zeros(bucket.shape, F32)
        for bkt in range(REL_BUCKETS):
            acc = jnp.where(hits[bkt], rb_ref[bkt, h], acc)
        o_ref[0, h] = jnp.where(causal, acc, neg)
        o_ref[1, h] = acc


def _bias_tables(rel_bias):
    assert WINDOW == BLOCK
    qi = jnp.arange(BLOCK)[:, None]
    kj = jnp.arange(BLOCK)[None, :]
    dist = jnp.where(kj <= qi, qi - kj, qi + BLOCK - kj)
    bucket = _rel_bucket(dist).astype(jnp.int32)
    out = pl.pallas_call(
        _bias_kernel,
        in_specs=[pl.BlockSpec(memory_space=pltpu.SMEM),
                  pl.BlockSpec(memory_space=pltpu.VMEM)],
        out_specs=pl.BlockSpec(memory_space=pltpu.VMEM),
        out_shape=jax.ShapeDtypeStruct((2, ATT_HEADS, BLOCK, BLOCK), F32),
        name="rel_bias_table",
    )(rel_bias, bucket)
    return out.reshape(2, ATT_KV_HEADS, ATT_GROUP * BLOCK, BLOCK)


def _swa_kernel(x_ref, xprev_ref, gpre_ref, gpost_ref, bias_ref, sink_ref, wqkv_hbm, wo_hbm, o_ref,
                kk_buf, vv_buf, att_buf, *scratch, staged):
    wqkv_ref, wo_ref = _load_staged(staged, (wqkv_hbm, wo_hbm), scratch, _first_step_2d())
    tm, d = x_ref.shape
    n_blk = tm // BLOCK
    dq = ATT_HEADS * ATT_HEAD_DIM
    dkv = ATT_KV_HEADS * ATT_HEAD_DIM
    first_tile = pl.program_id(1) == 0

    x = x_ref[...]
    h = _rms(x, gpre_ref[...]).astype(BF16)
    qkv = _dot(h, wqkv_ref[...])
    q = (qkv[:, :dq] * (ATT_HEAD_DIM ** -0.5)).astype(BF16)
    h_prev = _rms(xprev_ref[...], gpre_ref[...]).astype(BF16)
    kv_prev = _dot(h_prev, wqkv_ref[:, dq:])
    k = jnp.concatenate([kv_prev[:, :dkv], qkv[:, dq:dq + dkv]], axis=0)
    v = jnp.concatenate([kv_prev[:, dkv:], qkv[:, dq + dkv:]], axis=0)

    low = lax.broadcasted_iota(jnp.int32, k.shape, 1) < ATT_HEAD_DIM
    k_sw = pltpu.roll(k, ATT_HEAD_DIM, 1)
    v_sw = pltpu.roll(v, ATT_HEAD_DIM, 1)
    kk_buf[0] = jnp.where(low, k, k_sw).astype(BF16)
    kk_buf[1] = jnp.where(low, k_sw, k).astype(BF16)
    vv_buf[0, :, 0:LANES] = jnp.where(low, v, v_sw).astype(BF16)
    vv_buf[1, :, 0:LANES] = jnp.where(low, v_sw, v).astype(BF16)
    vv_buf[:, :, LANES:] = jnp.ones((ATT_KV_HEADS, tm + BLOCK, LANES), BF16)

    low_b = lax.broadcasted_iota(jnp.int32, (BLOCK, LANES), 1) < ATT_HEAD_DIM
    zero = jnp.zeros((BLOCK, LANES), BF16)
    pairs = ATT_GROUP // 2
    stacked = (2 * BLOCK, BLOCK)
    q_pos = lax.broadcasted_iota(jnp.int32, stacked, 0) & (BLOCK - 1)
    causal = lax.broadcasted_iota(jnp.int32, stacked, 1) <= q_pos
    zero_p = jnp.zeros(stacked, BF16)
    for n in range(n_blk):
        rows = slice(n * BLOCK, (n + 1) * BLOCK)
        variant = jnp.where(first_tile, 0, 1) if n == 0 else 1
        for hk in range(ATT_KV_HEADS):
            kb = kk_buf[hk, n * BLOCK:(n + 2) * BLOCK, :]
            vb = vv_buf[hk, n * BLOCK:(n + 2) * BLOCK, :]
            for j in range(pairs):
                col = (hk * pairs + j) * LANES
                pair_rows = slice(2 * j * BLOCK, (2 * j + 2) * BLOCK)
                q2 = q[rows, col:col + LANES]
                qs = jnp.concatenate([jnp.where(low_b, q2, zero), jnp.where(low_b, zero, q2)], axis=0)
                s2 = _dot_nt(qs, kb)
                logits = (jnp.where(causal, s2[:, BLOCK:], s2[:, :BLOCK])
                          + bias_ref[variant, hk, pair_rows, :])
                sink = sink_ref[hk, pair_rows, :]
                m = jnp.maximum(jnp.max(logits, axis=-1, keepdims=True), sink)
                e = jnp.exp(logits - m).astype(BF16)
                e2 = jnp.concatenate([jnp.where(causal, zero_p, e),
                                      jnp.where(causal, e, zero_p)], axis=1)
                r = _dot(e2, vb)
                out = r[:, :LANES] / (r[:, LANES:] + jnp.exp(sink - m))
                att_buf[rows, col:col + LANES] = jnp.where(low_b, out[:BLOCK], out[BLOCK:]).astype(BF16)

    y = _dot(att_buf[...], wo_ref[...])
    o_ref[...] = x + _rms(y, gpost_ref[...])


def _swa_mixer(x3, gains, row_pre, row_post, w_qkv, bias_tab, sinks, w_o, *, tm):
    b, s, d = x3.shape
    blocks_per_tile = tm // BLOCK
    tile = pl.BlockSpec((None, tm, d), lambda i, j: (i, j, 0))
    prev_block = pl.BlockSpec((None, BLOCK, d),
                              lambda i, j: (i, jnp.maximum(j * blocks_per_tile - 1, 0), 0))
    sink_col = jnp.repeat(sinks.astype(F32).reshape(ATT_KV_HEADS, ATT_GROUP), BLOCK, axis=1)
    sink_col = jnp.broadcast_to(sink_col[..., None], sink_col.shape + (LANES,))
    staged = (w_qkv, w_o)
    w_specs, w_args, w_scratch = _staged_specs(staged)
    return pl.pallas_call(
        functools.partial(_swa_kernel, staged=staged),
        grid=(b, s // tm),
        in_specs=[tile, prev_block, _gain_spec_d(row_pre, d), _gain_spec_d(row_post, d),
                  _resident(bias_tab.shape), _resident(sink_col.shape)] + w_specs,
        out_specs=tile,
        out_shape=jax.ShapeDtypeStruct((b, s, d), F32),
        scratch_shapes=[pltpu.VMEM((ATT_KV_HEADS, tm + BLOCK, LANES), BF16),
                        pltpu.VMEM((ATT_KV_HEADS, tm + BLOCK, 2 * LANES), BF16),
                        pltpu.VMEM((tm, ATT_HEADS * ATT_HEAD_DIM), BF16)] + w_scratch,
        compiler_params=_params(2),
        name="swa_mixer",
    )(x3, x3, gains, gains, bias_tab, sink_col, *w_args)


def _ret_tables(s, dk):
    f32 = F32
    half = dk // 2
    inv = ROPE_BASE ** (-jnp.arange(half, dtype=f32) / half)
    ang = jnp.arange(s)[:, None].astype(f32) * inv[None, :]
    c = RET_CHUNK
    log_g = jnp.log(1.0 - 2.0 ** (-5.0 - jnp.arange(RET_HEADS, dtype=f32)))
    idx = jnp.arange(c, dtype=f32)
    diff = idx[:, None] - idx[None, :]
    decay_mask = jnp.where(diff >= 0, jnp.exp(log_g[:, None, None] * jnp.maximum(diff, 0.0)), 0.0)
    q_decay = jnp.exp(log_g[:, None] * (idx + 1.0))[..., None]
    k_decay = jnp.exp(log_g[:, None] * (c - 1.0 - idx))[..., None]
    chunk_decay = jnp.exp(log_g * c)
    return jnp.cos(ang), jnp.sin(ang), decay_mask, q_decay, k_decay, chunk_decay


def _ret_kernel(x_ref, gpre_ref, gpost_ref, cos_ref, sin_ref, dmask_ref, qdec_ref, kdec_ref,
                cdec_ref, w_hbm, wo_hbm, o_ref, state_ref, gated_buf, *scratch,
                dk, dv, n_sub, staged):
    w_ref, wo_ref = _load_staged(staged, (w_hbm, wo_hbm), scratch, _first_step_2d())
    tm, d = x_ref.shape
    ts = tm // n_sub
    n_chunk = ts // RET_CHUNK
    half = dk // 2
    hq = RET_HEADS * dk

    @pl.when(pl.program_id(1) == 0)
    def _():
        state_ref[...] = jnp.zeros(state_ref.shape, F32)

    for sub in range(n_sub):
        r0 = sub * ts
        x = x_ref[r0:r0 + ts, :]
        h = _rms(x, gpre_ref[...]).astype(BF16)
        proj = _dot(h, w_ref[...])
        cos = cos_ref[r0:r0 + ts, :]
        sin = sin_ref[r0:r0 + ts, :]

        def rot(t):
            t1, t2 = t[:, :half], t[:, half:]
            return jnp.concatenate([t1 * cos - t2 * sin, t1 * sin + t2 * cos], axis=-1)

        for hd in range(RET_HEADS):
            q = rot(proj[:, hd * dk:(hd + 1) * dk])
            k = rot(proj[:, hq + hd * dk:hq + (hd + 1) * dk]) * (dk ** -0.5)
            v = proj[:, 2 * hq + hd * dv:2 * hq + (hd + 1) * dv]
            g = proj[:, 2 * hq + RET_HEADS * dv + hd * dv:2 * hq + RET_HEADS * dv + (hd + 1) * dv]
            dmask = dmask_ref[hd]
            qdec = qdec_ref[hd]
            kdec = kdec_ref[hd]
            cdec = cdec_ref[hd]
            state = state_ref[hd]
            for c in range(n_chunk):
                rows = slice(c * RET_CHUNK, (c + 1) * RET_CHUNK)
                qc = q[rows].astype(BF16)
                kc = k[rows]
                vc = v[rows].astype(BF16)
                inner = _dot_nt(qc, kc.astype(BF16)) * dmask
                lhs = jnp.concatenate([inner.astype(BF16), (q[rows] * qdec).astype(BF16)], axis=1)
                o = _dot(lhs, jnp.concatenate([vc, state.astype(BF16)], axis=0))
                state = state * cdec + _dot_tn((kc * kdec).astype(BF16), vc)
                mu = jnp.mean(o, axis=-1, keepdims=True)
                oc = o - mu
                var = jnp.mean(oc * oc, axis=-1, keepdims=True)
                on = oc * lax.rsqrt(var + NORM_EPS)
                gated_buf[r0 + c * RET_CHUNK:r0 + (c + 1) * RET_CHUNK, hd * dv:(hd + 1) * dv] = (
                    jax.nn.silu(g[rows]) * on).astype(BF16)
            state_ref[hd] = state

        y = _dot(gated_buf[r0:r0 + ts, :], wo_ref[...])
        o_ref[r0:r0 + ts, :] = x + _rms(y, gpost_ref[...])


def _ret_mixer(x3, gains, row_pre, row_post, w_qkvg, w_o, *, tm, n_sub):
    b, s, d = x3.shape
    dk = d // RET_HEADS
    dv = 2 * dk
    cos, sin, dmask, qdec, kdec, cdec = _ret_tables(s, dk)
    cdec = cdec.reshape(RET_HEADS, 1, 1)
    tile = pl.BlockSpec((None, tm, d), lambda i, j: (i, j, 0))
    rope = pl.BlockSpec((tm, dk // 2), lambda i, j: (j, 0))
    staged = (w_qkvg, w_o)
    w_specs, w_args, w_scratch = _staged_specs(staged)
    return pl.pallas_call(
        functools.partial(_ret_kernel, dk=dk, dv=dv, n_sub=n_sub, staged=staged),
        grid=(b, s // tm),
        in_specs=[tile, _gain_spec_d(row_pre, d), _gain_spec_d(row_post, d),
                  rope, rope, _resident(dmask.shape),
                  _resident(qdec.shape), _resident(kdec.shape), _resident(cdec.shape)] + w_specs,
        out_specs=tile,
        out_shape=jax.ShapeDtypeStruct((b, s, d), F32),
        scratch_shapes=[pltpu.VMEM((RET_HEADS, dk, dv), F32),
                        pltpu.VMEM((tm, RET_HEADS * dv), BF16)] + w_scratch,
        compiler_params=_params(2),
        name="ret_mixer",
    )(x3, gains, gains, cos, sin, dmask, qdec, kdec, cdec, *w_args)


def _tile(n, want):
    t = min(n, want)
    assert n % t == 0, (n, t)
    return t


def kernel(x, p, norm_g, ffn_w_gu, ffn_w_down, ple_w_proj, ple_w_gate, rel_bias, conv_w_in, conv_w,
           conv_w_out, swa_w_qkv, swa_sinks, swa_w_o, ret_w_qkvg, ret_w_o):
    depth = norm_g.shape[0]
    b, s, d = x.shape
    t = b * s
    gains = norm_g.astype(F32).reshape(depth * N_NORMS, 1, d)
    p3 = p.reshape(depth, t, p.shape[-1])
    bias_tab = _bias_tables(rel_bias.astype(F32)) if depth > 1 else None

    tm_ffn = _tile(t, 1024)
    tm_ple = _tile(t, 2048)
    tm_conv = _tile(s, 1024)
    tm_swa = _tile(s, 512)
    tm_ret = _tile(s, 512)

    for i in range(depth):
        kind, j = i % N_MIXERS, i // N_MIXERS
        r = i * N_NORMS
        x2 = _ffn(x.reshape(t, d), gains, r + 0, r + 1,
                  _Staged(ffn_w_gu, (i, 0)), _Staged(ffn_w_down, (i, 0)),
                  tm=tm_ffn, n_sub=4)
        x = x2.reshape(b, s, d)
        if kind == 0:
            x = _conv_mixer(x, gains, r + 2, r + 3, _Staged(conv_w_in, (j,)), conv_w[j].astype(F32),
                            _Staged(conv_w_out, (j,)), tm=tm_conv, n_sub=4)
        elif kind == 1:
            x = _swa_mixer(x, gains, r + 2, r + 3, _Staged(swa_w_qkv, (j,)), bias_tab, swa_sinks[j],
                           _Staged(swa_w_o, (j,)), tm=tm_swa)
        else:
            x = _ret_mixer(x, gains, r + 2, r + 3, _Staged(ret_w_qkvg, (j,)), _Staged(ret_w_o, (j,)),
                           tm=tm_ret, n_sub=2)
        x2 = _ffn(x.reshape(t, d), gains, r + 4, r + 5,
                  _Staged(ffn_w_gu, (i, 1)), _Staged(ffn_w_down, (i, 1)),
                  tm=tm_ffn, n_sub=4)
        x2 = _ple(x2, p3, i, gains, r + 6, _Staged(ple_w_gate, (i,)), _Staged(ple_w_proj, (i,)),
                  tm=tm_ple, n_sub=8)
        x = x2.reshape(b, s, d)
    return x
```
